```python
import math
import jax, jax.numpy as jnp
from jax import lax
import numpy as np

D_MODEL = 1024
BATCH = 4
SEQ = 8192
DEPTH = 1

CHUNK = 64
EPS = 1e-6

A_HEADS = 8
A_HEAD_DIM = 64
A_WIDTH = A_HEADS * A_HEAD_DIM
A_LEFT_CHUNKS = 8
A_BAND = (A_LEFT_CHUNKS + 1) * CHUNK
A_REL_CLIP = 256

B_HEADS = 8
B_HEAD_DIM = 64
B_WIDTH = B_HEADS * B_HEAD_DIM
IDX_HEADS = 8
IDX_DIM = 64
TOPK_MAX = 256
Q_BLOCK = 128

T5_BUCKETS = 32
T5_MAX_DIST = 128

SPLITS = (A_WIDTH, A_WIDTH, A_WIDTH, A_WIDTH,
          B_WIDTH, B_HEAD_DIM, B_HEAD_DIM, B_WIDTH,
          IDX_HEADS * IDX_DIM, IDX_DIM, IDX_HEADS,
          D_MODEL, D_MODEL)
IN_COLS = 4 * A_WIDTH + 2 * B_WIDTH + 2 * B_HEAD_DIM + IDX_HEADS * IDX_DIM + IDX_DIM + IDX_HEADS + 2 * D_MODEL

kernel_name = "hybrid_chunkband_dsa_gated_merge"

NEG = -1e30


def rms_norm(x, g):
    xf = x.astype(jnp.float32)
    y = xf * lax.rsqrt(jnp.mean(xf * xf, axis=-1, keepdims=True) + EPS)
    return (y * g.astype(jnp.float32)).astype(x.dtype)


def t5_bucket(rel):
    half = T5_BUCKETS // 2
    max_exact = half // 2
    ret = jnp.where(rel > 0, half, 0)
    n = jnp.abs(rel)
    nf = jnp.maximum(n, 1).astype(jnp.float32)
    large = max_exact + (jnp.log(nf / max_exact) / math.log(T5_MAX_DIST / max_exact)
                         * (half - max_exact)).astype(jnp.int32)
    large = jnp.minimum(large, half - 1)
    return ret + jnp.where(n < max_exact, n, large)


def chunk_band_attention(q, k, v, rel_bias):
    B, S, H, Dh = q.shape
    nc = S // CHUNK
    pad = A_LEFT_CHUNKS * CHUNK
    kp = jnp.pad(k, ((0, 0), (pad, 0), (0, 0), (0, 0))).astype(jnp.float32)
    vp = jnp.pad(v, ((0, 0), (pad, 0), (0, 0), (0, 0))).astype(jnp.float32)
    qc = q.astype(jnp.float32).reshape(B, nc, CHUNK, H, Dh).swapaxes(0, 1)
    i = jnp.arange(CHUNK)
    j = jnp.arange(A_BAND)
    rel = i[:, None] - (j[None, :] - pad)
    bias = rel_bias.astype(jnp.float32)[:, jnp.clip(rel, -A_REL_CLIP, A_REL_CLIP) + A_REL_CLIP]
    scale = Dh ** -0.5

    def one_chunk(args):
        qb, c = args
        kb = lax.dynamic_slice_in_dim(kp, c * CHUNK, A_BAND, axis=1)
        vb = lax.dynamic_slice_in_dim(vp, c * CHUNK, A_BAND, axis=1)
        s = jnp.einsum('bqhd,bkhd->bhqk', qb, kb) * scale + bias[None]
        valid = (c * CHUNK - pad + j) >= 0
        s = jnp.where(valid[None, None, None, :], s, NEG)
        p = jax.nn.softmax(s, axis=-1)
        return jnp.einsum('bhqk,bkhd->bqhd', p, vb)

    out = lax.map(one_chunk, (qc, jnp.arange(nc)))
    return out.swapaxes(0, 1).reshape(B, S, H * Dh).astype(q.dtype)


def dsa_attention(q, k, v, qi, ki, wi, t5_bias):
    B, S, H, Dh = q.shape
    n_sel = min(TOPK_MAX, S // 4)
    nb = S // Q_BLOCK
    key_chunk = jnp.arange(S) // CHUNK
    k32 = k.astype(jnp.float32)
    v32 = v.astype(jnp.float32)
    ki32 = ki.astype(jnp.float32)
    tb = t5_bias.astype(jnp.float32)
    scale = Dh ** -0.5
    idx_scale = (IDX_HEADS ** -0.5) * (IDX_DIM ** -0.5)

    def blocks(a):
        return a.reshape((B, nb, Q_BLOCK) + a.shape[2:]).swapaxes(0, 1)

    gather = jax.vmap(lambda kk, ss: kk[ss])

    def one_block(args):
        qb, qib, wib, blk = args
        qpos = blk * Q_BLOCK + jnp.arange(Q_BLOCK)
        qchunk = qpos // CHUNK
        admissible = key_chunk[None, :] <= qchunk[:, None]
        idx_logits = jnp.einsum('bqhd,bsd->bqhs', qib.astype(jnp.float32), ki32)
        score = jnp.einsum('bqhs,bqh->bqs', jax.nn.relu(idx_logits),
                           wib.astype(jnp.float32) * idx_scale)
        score = jnp.where(admissible[None], score, -jnp.inf)
        _, sel = lax.top_k(score, n_sel)
        sel_ok = (sel // CHUNK) <= qchunk[None, :, None]
        ks = gather(k32, sel)
        vs = gather(v32, sel)
        s = jnp.einsum('bqhd,bqkd->bqhk', qb.astype(jnp.float32), ks) * scale
        bias = tb[t5_bucket(sel - qpos[None, :, None])]
        s = s + jnp.swapaxes(bias, 2, 3)
        s = jnp.where(sel_ok[:, :, None, :], s, NEG)
        p = jax.nn.softmax(s, axis=-1)
        return jnp.einsum('bqhk,bqkd->bqhd', p, vs)

    out = lax.map(one_block, (blocks(q), blocks(qi), blocks(wi), jnp.arange(nb)))
    return out.swapaxes(0, 1).reshape(B, S, H * Dh).astype(q.dtype)


def setup_inputs(seed: int = 0) -> dict:
    key = jax.random.key(seed)
    ks = jax.random.split(key, 10)
    f32 = jnp.float32
    x = jax.random.normal(ks[0], (BATCH, SEQ, D_MODEL), f32)
    norm_gain = 1.0 + 0.01 * jax.random.normal(ks[1], (DEPTH, D_MODEL), f32)
    w_in = jax.random.normal(ks[2], (DEPTH, D_MODEL, IN_COLS), f32) * D_MODEL ** -0.5
    a_rel_bias = 0.1 * jax.random.normal(ks[3], (DEPTH, A_HEADS, 2 * A_REL_CLIP + 1), f32)
    t5_bias = 0.1 * jax.random.normal(ks[4], (T5_BUCKETS, B_HEADS), f32)
    w_a_out = jax.random.normal(ks[5], (DEPTH, A_WIDTH, D_MODEL), f32) * A_WIDTH ** -0.5
    w_b_out = jax.random.normal(ks[6], (DEPTH, B_WIDTH, D_MODEL), f32) * B_WIDTH ** -0.5
    w_out = jax.random.normal(ks[7], (DEPTH, D_MODEL, D_MODEL), f32) * D_MODEL ** -0.5
    final_gain = 1.0 + 0.01 * jax.random.normal(ks[8], (D_MODEL,), f32)
    return {"x": x, "norm_gain": norm_gain, "w_in": w_in, "a_rel_bias": a_rel_bias,
            "t5_bias": t5_bias, "w_a_out": w_a_out, "w_b_out": w_b_out,
            "w_out": w_out, "final_gain": final_gain}


def reference(x, norm_gain, w_in, a_rel_bias, t5_bias, w_a_out, w_b_out, w_out, final_gain):
    B, S, _ = x.shape
    offsets = [int(o) for o in np.cumsum(SPLITS)[:-1]]
    h = x
    for l in range(DEPTH):
        hn = rms_norm(h, norm_gain[l])
        proj = hn @ w_in[l]
        (qa, ka, va, za, qb, kb, vb, zb, qi, ki, wi, ga, gb) = jnp.split(proj, offsets, axis=-1)
        ya = chunk_band_attention(qa.reshape(B, S, A_HEADS, A_HEAD_DIM),
                                  ka.reshape(B, S, A_HEADS, A_HEAD_DIM),
                                  va.reshape(B, S, A_HEADS, A_HEAD_DIM),
                                  a_rel_bias[l]) * jax.nn.silu(za)
        yb = dsa_attention(qb.reshape(B, S, B_HEADS, B_HEAD_DIM), kb, vb,
                           qi.reshape(B, S, IDX_HEADS, IDX_DIM), ki, wi,
                           t5_bias) * jax.nn.silu(zb)
        merged = jax.nn.sigmoid(ga) * (ya @ w_a_out[l]) + jax.nn.sigmoid(gb) * (yb @ w_b_out[l])
        h = h + merged @ w_out[l]
    return rms_norm(h, final_gain)
```

```python
import math

import numpy as np
import jax
import jax.numpy as jnp
from jax import lax
from jax.experimental import pallas as pl
from jax.experimental.pallas import tpu as pltpu

F32 = jnp.float32
BF16 = jnp.bfloat16

CHUNK = 64
EPS = 1e-6
HEADS = 8
HEAD_DIM = 64
WIDTH = HEADS * HEAD_DIM
A_LEFT_CHUNKS = 8
A_REL_CLIP = 256
TOPK_MAX = 256
T5_BUCKETS = 32
T5_MAX_DIST = 128
NEG = -1e30
F32_MIN = float(np.finfo(np.float32).min)

V7X_VMEM_BYTES = 64 * 1024 * 1024
V7X_LANES = 128
VMEM_CAP_BYTES = 60000 * 1024
COMPILER_TEMP_BYTES = 8 * 1024 * 1024

BAND_TQ = 2 * CHUNK
BAND_WIN = (A_LEFT_CHUNKS + 2) * CHUNK
BAND_NBLK = BAND_WIN // BAND_TQ
DSA_T = 256
BISECT_MAX_ITERS = 48


def _vmem_limit(block_bytes):
    return int(min(block_bytes + COMPILER_TEMP_BYTES, VMEM_CAP_BYTES))


def _nbytes(shape, dtype):
    return int(np.prod(shape)) * jnp.dtype(dtype).itemsize


def _t5_bucket(rel):
    half = T5_BUCKETS // 2
    max_exact = half // 2
    ret = jnp.where(rel > 0, half, 0)
    n = jnp.abs(rel)
    nf = jnp.maximum(n, 1).astype(F32)
    large = max_exact + (jnp.log(nf / max_exact) / math.log(T5_MAX_DIST / max_exact)
                         * (half - max_exact)).astype(jnp.int32)
    large = jnp.minimum(large, half - 1)
    return ret + jnp.where(n < max_exact, n, large)


def _sigmoid(v):
    return 1.0 / (1.0 + jnp.exp(-v))


def _proj_kernel(x_ref, g_ref, wn_ref, wg_ref, wt_ref,
                 qkv_ref, ki_ref, kb_ref, gate_ref, qbT_ref, qiT_ref, vbT_ref, wiT_ref):
    xf = x_ref[...]
    ms = jnp.mean(xf * xf, axis=-1, keepdims=True)
    hn = ((xf * lax.rsqrt(ms + EPS)) * g_ref[...]).astype(BF16)

    n_qkv = qkv_ref.shape[1]
    for c0 in range(0, n_qkv, WIDTH):
        qkv_ref[:, c0:c0 + WIDTH] = jnp.dot(
            hn, wn_ref[:, c0:c0 + WIDTH], preferred_element_type=F32).astype(BF16)
    kk = jnp.dot(hn, wn_ref[:, n_qkv:n_qkv + 2 * HEAD_DIM], preferred_element_type=F32)
    ki_ref[...] = kk[:, :HEAD_DIM].astype(BF16)
    kb_ref[...] = kk[:, HEAD_DIM:].astype(BF16)

    n_gate = gate_ref.shape[1]
    for c0 in range(0, n_gate, WIDTH):
        gate_ref[:, c0:c0 + WIDTH] = jnp.dot(
            hn, wg_ref[:, c0:c0 + WIDTH], preferred_element_type=F32)

    nt = (((1,), (1,)), ((), ()))
    qbT_ref[...] = lax.dot_general(wt_ref[0:WIDTH, :], hn, nt,
                                   preferred_element_type=F32).astype(BF16)
    qiT_ref[...] = lax.dot_general(wt_ref[WIDTH:2 * WIDTH, :], hn, nt,
                                   preferred_element_type=F32).astype(BF16)
    tail = lax.dot_general(wt_ref[2 * WIDTH:, :], hn, nt, preferred_element_type=F32)
    n_sub = vbT_ref.shape[0]
    for s in range(n_sub):
        vbT_ref[s] = tail[0:HEAD_DIM, s * DSA_T:(s + 1) * DSA_T].astype(BF16)
    wiT_ref[...] = tail[HEAD_DIM:HEAD_DIM + HEADS, :]


def _proj(x2, gain, wn, wg, wt, tm):
    n, d = x2.shape
    n_qkv = 3 * WIDTH
    n_gate = wg.shape[1]
    n_sub = tm // DSA_T
    grid = (n // tm,)
    const = lambda i: (0, 0)
    in_specs = [
        pl.BlockSpec((tm, d), lambda i: (i, 0)),
        pl.BlockSpec((1, d), const),
        pl.BlockSpec(wn.shape, const),
        pl.BlockSpec(wg.shape, const),
        pl.BlockSpec(wt.shape, const),
    ]
    out_shape = (
        jax.ShapeDtypeStruct((n, n_qkv), BF16),
        jax.ShapeDtypeStruct((n, HEAD_DIM), BF16),
        jax.ShapeDtypeStruct((n, HEAD_DIM), BF16),
        jax.ShapeDtypeStruct((n, n_gate), F32),
        jax.ShapeDtypeStruct((WIDTH, n), BF16),
        jax.ShapeDtypeStruct((WIDTH, n), BF16),
        jax.ShapeDtypeStruct((n // DSA_T, HEAD_DIM, DSA_T), BF16),
        jax.ShapeDtypeStruct((HEADS, n), F32),
    )
    out_specs = (
        pl.BlockSpec((tm, n_qkv), lambda i: (i, 0)),
        pl.BlockSpec((tm, HEAD_DIM), lambda i: (i, 0)),
        pl.BlockSpec((tm, HEAD_DIM), lambda i: (i, 0)),
        pl.BlockSpec((tm, n_gate), lambda i: (i, 0)),
        pl.BlockSpec((WIDTH, tm), lambda i: (0, i)),
        pl.BlockSpec((WIDTH, tm), lambda i: (0, i)),
        pl.BlockSpec((n_sub, HEAD_DIM, DSA_T), lambda i: (i, 0, 0)),
        pl.BlockSpec((HEADS, tm), lambda i: (0, i)),
    )
    blk = 2 * (_nbytes((tm, d), F32) + _nbytes(wn.shape, BF16) + _nbytes(wg.shape, BF16)
               + _nbytes(wt.shape, BF16) + _nbytes((tm, n_qkv), BF16)
               + 2 * _nbytes((tm, V7X_LANES), BF16) + _nbytes((tm, n_gate), F32)
               + 2 * _nbytes((WIDTH, tm), BF16) + _nbytes((HEAD_DIM, tm), BF16)
               + _nbytes((HEADS, tm), F32))
    return pl.pallas_call(
        _proj_kernel, out_shape=out_shape, grid=grid, in_specs=in_specs, out_specs=out_specs,
        name="proj",
        compiler_params=pltpu.CompilerParams(
            dimension_semantics=("arbitrary",), vmem_limit_bytes=_vmem_limit(blk)),
    )(x2, gain, wn, wg, wt)


def _band_kernel(q_ref, k_ref, v_ref, bias_ref, o_ref):
    t = pl.program_id(1)
    lane = lax.broadcasted_iota(jnp.int32, (BAND_TQ, V7X_LANES), 1)
    first_half = lane < HEAD_DIM
    nt = (((1,), (1,)), ((), ()))
    for p in range(HEADS // 2):
        cs = slice(V7X_LANES * p, V7X_LANES * (p + 1))
        qp = q_ref[:, cs] * (HEAD_DIM ** -0.5)
        starts = []
        for i in range(BAND_NBLK):
            kidx = t - (BAND_NBLK - 1) + i
            starts.append((kidx, pl.multiple_of(jnp.maximum(kidx, 0) * BAND_TQ, BAND_TQ)))
        outs = []
        for e in range(2):
            h = 2 * p + e
            qe = jnp.where(first_half if e == 0 else jnp.logical_not(first_half), qp,
                           jnp.zeros_like(qp))
            s_blocks = []
            for i, (kidx, ks) in enumerate(starts):
                kblk = k_ref[pl.ds(ks, BAND_TQ), cs]
                s = lax.dot_general(qe, kblk, nt, preferred_element_type=F32)
                s = s + bias_ref[h, :, BAND_TQ * i:BAND_TQ * (i + 1)]
                if i < BAND_NBLK - 1:
                    s = jnp.where(kidx >= 0, s, NEG)
                s_blocks.append(s)
            m_el = s_blocks[0]
            for s in s_blocks[1:]:
                m_el = jnp.maximum(m_el, s)
            m = jnp.max(m_el, axis=-1, keepdims=True)
            l_el = None
            acc = None
            for (kidx, ks), s in zip(starts, s_blocks):
                pr = jnp.exp(s - m)
                l_el = pr if l_el is None else l_el + pr
                vblk = v_ref[pl.ds(ks, BAND_TQ), cs]
                pv = jnp.dot(pr.astype(BF16), vblk, preferred_element_type=F32)
                acc = pv if acc is None else acc + pv
            l = jnp.sum(l_el, axis=-1, keepdims=True)
            outs.append(acc * (1.0 / l))
        o_ref[:, cs] = jnp.where(first_half, outs[0], outs[1])


def _band(qkv3, abias):
    b, s, _ = qkv3.shape
    grid = (b, s // BAND_TQ)
    in_specs = [
        pl.BlockSpec((None, BAND_TQ, WIDTH), lambda bi, t: (bi, t, 0)),
        pl.BlockSpec((None, s, WIDTH), lambda bi, t: (bi, 0, 1)),
        pl.BlockSpec((None, s, WIDTH), lambda bi, t: (bi, 0, 2)),
        pl.BlockSpec(abias.shape, lambda bi, t: (0, 0, 0)),
    ]
    out_specs = pl.BlockSpec((None, BAND_TQ, WIDTH), lambda bi, t: (bi, t, 0))
    blk = 2 * (_nbytes((BAND_TQ, WIDTH), BF16) + 2 * _nbytes((s, WIDTH), BF16)
               + _nbytes(abias.shape, F32) + _nbytes((BAND_TQ, WIDTH), F32))
    return pl.pallas_call(
        _band_kernel, out_shape=jax.ShapeDtypeStruct((b, s, WIDTH), F32),
        grid=grid, in_specs=in_specs, out_specs=out_specs, name="band",
        compiler_params=pltpu.CompilerParams(
            dimension_semantics=("arbitrary", "arbitrary"), vmem_limit_bytes=_vmem_limit(blk)),
    )(qkv3, qkv3, qkv3, abias)


def _make_dsa_kernel(n_sel):
    T = DSA_T
    kf = float(n_sel)
    idx_scale = (HEADS ** -0.5) * (HEAD_DIM ** -0.5)

    def kernel(qbT_ref, qiT_ref, wiT_ref, ki_ref, kb_ref, vbT_ref, gbias_ref, adm_ref, o_ref,
               sc_ref, m_ref, l_ref, acc_ref):
        j = pl.program_id(1)
        nt = j + 1
        zrow = jnp.zeros((1, T), F32)

        def tile_rows(kt):
            return pl.ds(pl.multiple_of(kt * T, T), T)

        wis = wiT_ref[...] * idx_scale

        def p1(kt, carry):
            mn, mx = carry
            ki_t = ki_ref[tile_rows(kt), :]
            acc = jnp.zeros((T, T), F32)
            for h in range(HEADS):
                lg = jnp.dot(ki_t, qiT_ref[HEAD_DIM * h:HEAD_DIM * (h + 1), :],
                             preferred_element_type=F32)
                acc = acc + jnp.maximum(lg, 0.0) * wis[h:h + 1, :]
            mn = jnp.minimum(mn, jnp.min(acc, axis=0, keepdims=True))
            mx = jnp.maximum(mx, jnp.max(acc, axis=0, keepdims=True))
            is_diag = (kt == j).astype(jnp.int32)
            sc_ref[tile_rows(kt), :] = acc + adm_ref[is_diag]
            return mn, mx

        lo0, hi0 = lax.fori_loop(0, nt, p1, (jnp.full((1, T), jnp.inf, F32),
                                             jnp.full((1, T), -jnp.inf, F32)))

        qpos = j * T + lax.broadcasted_iota(jnp.int32, (1, T), 1)
        n_adm = ((qpos // CHUNK + 1) * CHUNK).astype(F32)
        small = n_adm <= kf

        def count_ge(th):
            def body(kt, c):
                blk = sc_ref[tile_rows(kt), :]
                return c + jnp.sum(jnp.where(blk >= th, 1.0, 0.0), axis=0, keepdims=True)
            return lax.fori_loop(0, nt, body, zrow)

        def any_open(done):
            return jnp.max(jnp.where(done, 0.0, 1.0)) > 0.0

        def b_cond(st):
            it, _, _, _, open_ = st
            return jnp.logical_and(it < BISECT_MAX_ITERS, open_)

        def b_body(st):
            it, lo, hi, cl, _ = st
            mid = 0.5 * lo + 0.5 * hi
            c = count_ge(mid)
            ge = c >= kf
            lo = jnp.where(ge, mid, lo)
            cl = jnp.where(ge, c, cl)
            hi = jnp.where(ge, hi, mid)
            return it + 1, lo, hi, cl, any_open(jnp.logical_or(small, cl == kf))

        _, lo, hi, cl, open_ = lax.while_loop(
            b_cond, b_body,
            (jnp.int32(0), lo0, hi0, n_adm, any_open(jnp.logical_or(small, n_adm == kf))))

        def fast_mask():
            thr = jnp.where(small, F32_MIN, lo)

            def body(kt, _):
                blk = sc_ref[tile_rows(kt), :]
                sc_ref[tile_rows(kt), :] = jnp.where(blk >= thr, 0.0, NEG)
                return 0
            lax.fori_loop(0, nt, body, 0)

        def exact_mask():
            def done_of(lo_, hi_, cl_):
                return jnp.logical_or(jnp.logical_or(small, cl_ == kf), lo_ >= hi_)

            def s_body(st):
                lo_, hi_, cl_, _ = st
                done = done_of(lo_, hi_, cl_)
                mid = 0.5 * lo_ + 0.5 * hi_
                mid = jnp.where(mid > lo_, mid, hi_)

                def body(kt, c3):
                    c, mn_ge, mx_lt = c3
                    blk = sc_ref[tile_rows(kt), :]
                    ge_ = blk >= mid
                    c = c + jnp.sum(jnp.where(ge_, 1.0, 0.0), axis=0, keepdims=True)
                    mn_ge = jnp.minimum(mn_ge, jnp.min(jnp.where(ge_, blk, jnp.inf),
                                                       axis=0, keepdims=True))
                    mx_lt = jnp.maximum(mx_lt, jnp.max(jnp.where(ge_, -jnp.inf, blk),
                                                       axis=0, keepdims=True))
                    return c, mn_ge, mx_lt
                c, mn_ge, mx_lt = lax.fori_loop(
                    0, nt, body, (zrow, jnp.full((1, T), jnp.inf, F32),
                                  jnp.full((1, T), -jnp.inf, F32)))
                ge = c >= kf
                lo_n = jnp.where(done, lo_, jnp.where(ge, mn_ge, lo_))
                cl_n = jnp.where(done, cl_, jnp.where(ge, c, cl_))
                hi_n = jnp.where(done, hi_, jnp.where(ge, hi_, mx_lt))
                return lo_n, hi_n, cl_n, any_open(done_of(lo_n, hi_n, cl_n))

            lo_e, _, _, _ = lax.while_loop(
                lambda st: st[3], s_body, (lo, hi, cl, any_open(done_of(lo, hi, cl))))
            thr = jnp.where(small, F32_MIN, lo_e)

            def gt_body(kt, c):
                blk = sc_ref[tile_rows(kt), :]
                return c + jnp.sum(jnp.where(blk > thr, 1.0, 0.0), axis=0, keepdims=True)
            need = kf - lax.fori_loop(0, nt, gt_body, zrow)

            r_io = lax.broadcasted_iota(jnp.int32, (T, T), 0)
            c_io = lax.broadcasted_iota(jnp.int32, (T, T), 1)
            ltri = jnp.where(c_io < r_io, 1.0, 0.0).astype(BF16)

            def m_body(kt, carry):
                blk = sc_ref[tile_rows(kt), :]
                eq = jnp.where(blk == thr, 1.0, 0.0)
                rank = jnp.dot(ltri, eq.astype(BF16), preferred_element_type=F32) + carry
                take = jnp.logical_or(blk > thr,
                                      jnp.logical_and(blk == thr, rank < need))
                sc_ref[tile_rows(kt), :] = jnp.where(take, 0.0, NEG)
                return carry + jnp.sum(eq, axis=0, keepdims=True)
            lax.fori_loop(0, nt, m_body, zrow)

        lax.cond(open_, exact_mask, fast_mask)

        m_ref[...] = jnp.full(m_ref.shape, NEG, F32)
        l_ref[...] = jnp.zeros(l_ref.shape, F32)
        acc_ref[...] = jnp.zeros(acc_ref.shape, F32)
        qs = qbT_ref[...] * (HEAD_DIM ** -0.5)

        def p3(kt, _):
            kb_t = kb_ref[tile_rows(kt), :]
            vT_t = vbT_ref[kt]
            msk = sc_ref[tile_rows(kt), :]
            slab = jnp.clip(kt - (j - 2), 0, 2)
            for h in range(HEADS):
                rows = slice(HEAD_DIM * h, HEAD_DIM * (h + 1))
                s = jnp.dot(kb_t, qs[rows, :], preferred_element_type=F32)
                s = s + (msk + gbias_ref[slab, h])
                m_old = m_ref[h:h + 1, :]
                m_new = jnp.maximum(m_old, jnp.max(s, axis=0, keepdims=True))
                alpha = jnp.exp(m_old - m_new)
                pr = jnp.exp(s - m_new)
                l_ref[h:h + 1, :] = alpha * l_ref[h:h + 1, :] + jnp.sum(pr, axis=0, keepdims=True)
                acc_ref[rows, :] = alpha * acc_ref[rows, :] + jnp.dot(
                    vT_t, pr.astype(BF16), preferred_element_type=F32)
                m_ref[h:h + 1, :] = m_new
            return 0

        lax.fori_loop(0, nt, p3, 0)

        inv = 1.0 / l_ref[...]
        for h in range(HEADS):
            rows = slice(HEAD_DIM * h, HEAD_DIM * (h + 1))
            acc_ref[rows, :] = acc_ref[rows, :] * inv[h:h + 1, :]
        o_ref[...] = acc_ref[...].T

    return kernel


def _dsa(qbT, qiT, wiT, ki3, kb3, vbT4, gbias, adm, n_sel):
    b, s, _ = ki3.shape
    T = DSA_T
    nq = s // T
    grid = (b, nq)
    in_specs = [
        pl.BlockSpec((WIDTH, T), lambda bi, j: (0, bi * nq + j)),
        pl.BlockSpec((WIDTH, T), lambda bi, j: (0, bi * nq + j)),
        pl.BlockSpec((HEADS, T), lambda bi, j: (0, bi * nq + j)),
        pl.BlockSpec((None, s, HEAD_DIM), lambda bi, j: (bi, 0, 0)),
        pl.BlockSpec((None, s, HEAD_DIM), lambda bi, j: (bi, 0, 0)),
        pl.BlockSpec((None, nq, HEAD_DIM, T), lambda bi, j: (bi, 0, 0, 0)),
        pl.BlockSpec(gbias.shape, lambda bi, j: (0, 0, 0, 0)),
        pl.BlockSpec(adm.shape, lambda bi, j: (0, 0, 0)),
    ]
    out_specs = pl.BlockSpec((None, T, WIDTH), lambda bi, j: (bi, j, 0))
    scratch = [
        pltpu.VMEM((s, T), F32),
        pltpu.VMEM((HEADS, T), F32),
        pltpu.VMEM((HEADS, T), F32),
        pltpu.VMEM((WIDTH, T), F32),
    ]
    blk = (2 * (2 * _nbytes((WIDTH, T), BF16) + _nbytes((HEADS, T), F32)
                + 2 * _nbytes((s, V7X_LANES), BF16) + _nbytes((HEAD_DIM, s), BF16)
                + _nbytes(gbias.shape, F32) + _nbytes(adm.shape, F32)
                + _nbytes((T, WIDTH), F32))
           + _nbytes((s, T), F32) + 2 * _nbytes((HEADS, T), F32) + _nbytes((WIDTH, T), F32))
    return pl.pallas_call(
        _make_dsa_kernel(n_sel), out_shape=jax.ShapeDtypeStruct((b, s, WIDTH), F32),
        grid=grid, in_specs=in_specs, out_specs=out_specs, scratch_shapes=scratch, name="dsa",
        compiler_params=pltpu.CompilerParams(
            dimension_semantics=("arbitrary", "arbitrary"), vmem_limit_bytes=_vmem_limit(blk)),
    )(qbT, qiT, wiT, ki3, kb3, vbT4, gbias, adm)


def _merge_kernel(x_ref, ya_ref, yb_ref, gate_ref, wa_ref, wb_ref, wo_ref, fg_ref, o_ref):
    d = x_ref.shape[1]
    za = gate_ref[:, 0:WIDTH]
    zb = gate_ref[:, WIDTH:2 * WIDTH]
    ga = gate_ref[:, 2 * WIDTH:2 * WIDTH + d]
    gb = gate_ref[:, 2 * WIDTH + d:2 * WIDTH + 2 * d]
    ua = (ya_ref[...] * (za * _sigmoid(za))).astype(BF16)
    ub = (yb_ref[...] * (zb * _sigmoid(zb))).astype(BF16)
    pa = jnp.dot(ua, wa_ref[...], preferred_element_type=F32)
    pb = jnp.dot(ub, wb_ref[...], preferred_element_type=F32)
    merged = _sigmoid(ga) * pa + _sigmoid(gb) * pb
    h = x_ref[...] + jnp.dot(merged.astype(BF16), wo_ref[...], preferred_element_type=F32)
    ms = jnp.mean(h * h, axis=-1, keepdims=True)
    o_ref[...] = (h * lax.rsqrt(ms + EPS)) * fg_ref[...]


def _merge(x2, ya2, yb2, gates, wa, wb, wo, fg, tm):
    n, d = x2.shape
    n_gate = gates.shape[1]
    grid = (n // tm,)
    row = lambda i: (i, 0)
    const = lambda i: (0, 0)
    in_specs = [
        pl.BlockSpec((tm, d), row),
        pl.BlockSpec((tm, WIDTH), row),
        pl.BlockSpec((tm, WIDTH), row),
        pl.BlockSpec((tm, n_gate), row),
        pl.BlockSpec(wa.shape, const),
        pl.BlockSpec(wb.shape, const),
        pl.BlockSpec(wo.shape, const),
        pl.BlockSpec((1, d), const),
    ]
    blk = 2 * (2 * _nbytes((tm, d), F32) + 2 * _nbytes((tm, WIDTH), F32)
               + _nbytes((tm, n_gate), F32) + _nbytes(wa.shape, BF16) + _nbytes(wb.shape, BF16)
               + _nbytes(wo.shape, BF16))
    return pl.pallas_call(
        _merge_kernel, out_shape=jax.ShapeDtypeStruct((n, d), F32),
        grid=grid, in_specs=in_specs, out_specs=pl.BlockSpec((tm, d), row), name="merge",
        compiler_params=pltpu.CompilerParams(
            dimension_semantics=("arbitrary",), vmem_limit_bytes=_vmem_limit(blk)),
    )(x2, ya2, yb2, gates, wa, wb, wo, fg)


def _band_bias(a_rel_bias):
    qq = jnp.arange(BAND_TQ)[:, None]
    jj = jnp.arange(BAND_WIN)[None, :]
    rel = qq + A_LEFT_CHUNKS * CHUNK - jj
    bias = a_rel_bias.astype(F32)[:, jnp.clip(rel, -A_REL_CLIP, A_REL_CLIP) + A_REL_CLIP]
    in_band = jnp.logical_and(jj // CHUNK >= qq // CHUNK,
                              jj // CHUNK <= qq // CHUNK + A_LEFT_CHUNKS)
    return jnp.where(in_band[None], bias, NEG)


def _dsa_bias(t5_bias):
    T = DSA_T
    far = T5_BUCKETS // 2 - 1
    kk = jnp.arange(T, dtype=jnp.int32)[:, None]
    qq = jnp.arange(T, dtype=jnp.int32)[None, :]
    tb = t5_bias.astype(F32)
    slabs = [jnp.zeros((HEADS, T, T), F32)]
    for off in (-T, 0):
        g = tb[_t5_bucket(kk + off - qq)] - tb[far]
        slabs.append(jnp.transpose(g, (2, 0, 1)))
    return jnp.stack(slabs)


def _far_bucket_is_constant(s):
    half = T5_BUCKETS // 2
    max_exact = half // 2
    for n in (DSA_T + 1, max(s - 1, DSA_T + 1)):
        large = max_exact + int(math.log(n / max_exact) / math.log(T5_MAX_DIST / max_exact)
                                * (half - max_exact))
        if min(large, half - 1) != half - 1:
            return False
    return True


def kernel(x, norm_gain, w_in, a_rel_bias, t5_bias, w_a_out, w_b_out, w_out, final_gain):
    b, s, d = x.shape
    depth = w_in.shape[0]
    n = b * s
    assert s % DSA_T == 0 and s % BAND_TQ == 0
    assert _far_bucket_is_constant(s)
    tm = 512 if n % 512 == 0 else DSA_T
    n_sel = min(TOPK_MAX, s // 4)

    splits = (WIDTH,) * 4 + (WIDTH, HEAD_DIM, HEAD_DIM, WIDTH) + (WIDTH, HEAD_DIM, HEADS) + (d, d)
    offs = np.concatenate([[0], np.cumsum(splits)])
    assert offs[-1] == w_in.shape[2]

    abias = _band_bias
    gbias = _dsa_bias(t5_bias)
    kk = jnp.arange(DSA_T)[:, None] // CHUNK
    qq = jnp.arange(DSA_T)[None, :] // CHUNK
    adm = jnp.stack([jnp.zeros((DSA_T, DSA_T), F32),
                     jnp.where(kk <= qq, 0.0, -jnp.inf).astype(F32)])

    h2 = x.reshape(n, d)
    for l in range(depth):
        w = w_in[l]
        (qa, ka, va, za, qb, kb, vb, zb, qi, ki, wi, ga, gb) = [
            w[:, offs[i]:offs[i + 1]] for i in range(len(splits))]
        wn = jnp.concatenate([qa, ka, va, ki, kb], axis=1).astype(BF16)
        wg = jnp.concatenate([za, zb, ga, gb], axis=1).astype(BF16)
        wt = jnp.concatenate([qb, qi, vb, wi, jnp.zeros((d, HEADS), w.dtype)],
                             axis=1).T.astype(BF16)

        qkv, ki_o, kb_o, gates, qbT, qiT, vbT, wiT = _proj(
            h2, norm_gain[l].reshape(1, d).astype(F32), wn, wg, wt, tm)

        ya = _band(qkv.reshape(b, s, 3 * WIDTH), abias(a_rel_bias[l]))
        yb = _dsa(qbT, qiT, wiT, ki_o.reshape(b, s, HEAD_DIM), kb_o.reshape(b, s, HEAD_DIM),
                  vbT.reshape(b, s // DSA_T, HEAD_DIM, DSA_T), gbias, adm, n_sel)

        assert depth == 1
        h2 = _merge(h2, ya.reshape(n, WIDTH), yb.reshape(n, WIDTH), gates,
                    w_a_out[l].astype(BF16), w_b_out[l].astype(BF16), w_out[l].astype(BF16),
                    final_gain.reshape(1, d).astype(F32), tm)
    return h2.reshape(b, s, d)
```

```python
import math

import numpy as np
import jax
import jax.numpy as jnp
from jax import lax
from jax.experimental import pallas as pl
from jax.experimental.pallas import tpu as pltpu

F32 = jnp.float32
BF16 = jnp.bfloat16

CHUNK = 64
EPS = 1e-6
HEADS = 8
HEAD_DIM = 64
WIDTH = HEADS * HEAD_DIM
A_LEFT_CHUNKS = 8
A_REL_CLIP = 256
TOPK_MAX = 256
T5_BUCKETS = 32
T5_MAX_DIST = 128
NEG = -1e30
F32_MIN = float(np.finfo(np.float32).min)

V7X_VMEM_BYTES = 64 * 1024 * 1024
V7X_LANES = 128
VMEM_CAP_BYTES = 60000 * 1024
COMPILER_TEMP_BYTES = 8 * 1024 * 1024

BAND_TQ = 2 * CHUNK
BAND_WIN = (A_LEFT_CHUNKS + 2) * CHUNK
BAND_NBLK = BAND_WIN // BAND_TQ
DSA_T = 256
BISECT_MAX_ITERS = 48


def _vmem_limit(block_bytes):
    return int(min(block_bytes + COMPILER_TEMP_BYTES, VMEM_CAP_BYTES))


def _nbytes(shape, dtype):
    return int(np.prod(shape)) * jnp.dtype(dtype).itemsize


def _t5_bucket(rel):
    half = T5_BUCKETS // 2
    max_exact = half // 2
    ret = jnp.where(rel > 0, half, 0)
    n = jnp.abs(rel)
    nf = jnp.maximum(n, 1).astype(F32)
    large = max_exact + (jnp.log(nf / max_exact) / math.log(T5_MAX_DIST / max_exact)
                         * (half - max_exact)).astype(jnp.int32)
    large = jnp.minimum(large, half - 1)
    return ret + jnp.where(n < max_exact, n, large)


def _sigmoid(v):
    return 1.0 / (1.0 + jnp.exp(-v))


def _proj_kernel(x_ref, g_ref, wn_ref, wg_ref, wt_ref,
                 qkv_ref, ki_ref, kb_ref, gate_ref, qbT_ref, qiT_ref, vbT_ref, wiT_ref):
    xf = x_ref[...]
    ms = jnp.mean(xf * xf, axis=-1, keepdims=True)
    hn = ((xf * lax.rsqrt(ms + EPS)) * g_ref[...]).astype(BF16)

    n_qkv = qkv_ref.shape[1]
    for c0 in range(0, n_qkv, WIDTH):
        qkv_ref[:, c0:c0 + WIDTH] = jnp.dot(
            hn, wn_ref[:, c0:c0 + WIDTH], preferred_element_type=F32).astype(BF16)
    kk = jnp.dot(hn, wn_ref[:, n_qkv:n_qkv + 2 * HEAD_DIM], preferred_element_type=F32)
    ki_ref[...] = kk[:, :HEAD_DIM].astype(BF16)
    kb_ref[...] = kk[:, HEAD_DIM:].astype(BF16)

    n_gate = gate_ref.shape[1]
    for c0 in range(0, n_gate, WIDTH):
        gate_ref[:, c0:c0 + WIDTH] = jnp.dot(
            hn, wg_ref[:, c0:c0 + WIDTH], preferred_element_type=F32)

    nt = (((1,), (1,)), ((), ()))
    qbT_ref[...] = lax.dot_general(wt_ref[0:WIDTH, :], hn, nt,
                                   preferred_element_type=F32).astype(BF16)
    qiT_ref[...] = lax.dot_general(wt_ref[WIDTH:2 * WIDTH, :], hn, nt,
                                   preferred_element_type=F32).astype(BF16)
    tail = lax.dot_general(wt_ref[2 * WIDTH:, :], hn, nt, preferred_element_type=F32)
    n_sub = vbT_ref.shape[0]
    for s in range(n_sub):
        vbT_ref[s] = tail[0:HEAD_DIM, s * DSA_T:(s + 1) * DSA_T].astype(BF16)
    wiT_ref[...] = tail[HEAD_DIM:HEAD_DIM + HEADS, :]


def _proj(x2, gain, wn, wg, wt, tm):
    n, d = x2.shape
    n_qkv = 3 * WIDTH
    n_gate = wg.shape[1]
    n_sub = tm // DSA_T
    grid = (n // tm,)
    const = lambda i: (0, 0)
    in_specs = [
        pl.BlockSpec((tm, d), lambda i: (i, 0)),
        pl.BlockSpec((1, d), const),
        pl.BlockSpec(wn.shape, const),
        pl.BlockSpec(wg.shape, const),
        pl.BlockSpec(wt.shape, const),
    ]
    out_shape = (
        jax.ShapeDtypeStruct((n, n_qkv), BF16),
        jax.ShapeDtypeStruct((n, HEAD_DIM), BF16),
        jax.ShapeDtypeStruct((n, HEAD_DIM), BF16),
        jax.ShapeDtypeStruct((n, n_gate), F32),
        jax.ShapeDtypeStruct((WIDTH, n), BF16),
        jax.ShapeDtypeStruct((WIDTH, n), BF16),
        jax.ShapeDtypeStruct((n // DSA_T, HEAD_DIM, DSA_T), BF16),
        jax.ShapeDtypeStruct((HEADS, n), F32),
    )
    out_specs = (
        pl.BlockSpec((tm, n_qkv), lambda i: (i, 0)),
        pl.BlockSpec((tm, HEAD_DIM), lambda i: (i, 0)),
        pl.BlockSpec((tm, HEAD_DIM), lambda i: (i, 0)),
        pl.BlockSpec((tm, n_gate), lambda i: (i, 0)),
        pl.BlockSpec((WIDTH, tm), lambda i: (0, i)),
        pl.BlockSpec((WIDTH, tm), lambda i: (0, i)),
        pl.BlockSpec((n_sub, HEAD_DIM, DSA_T), lambda i: (i, 0, 0)),
        pl.BlockSpec((HEADS, tm), lambda i: (0, i)),
    )
    blk = 2 * (_nbytes((tm, d), F32) + _nbytes(wn.shape, BF16) + _nbytes(wg.shape, BF16)
               + _nbytes(wt.shape, BF16) + _nbytes((tm, n_qkv), BF16)
               + 2 * _nbytes((tm, V7X_LANES), BF16) + _nbytes((tm, n_gate), F32)
               + 2 * _nbytes((WIDTH, tm), BF16) + _nbytes((HEAD_DIM, tm), BF16)
               + _nbytes((HEADS, tm), F32))
    return pl.pallas_call(
        _proj_kernel, out_shape=out_shape, grid=grid, in_specs=in_specs, out_specs=out_specs,
        name="proj",
        compiler_params=pltpu.CompilerParams(
            dimension_semantics=("arbitrary",), vmem_limit_bytes=_vmem_limit(blk)),
    )(x2, gain, wn, wg, wt)


def _band_kernel(q_ref, k_ref, v_ref, bias_ref, o_ref):
    t = pl.program_id(1)
    lane = lax.broadcasted_iota(jnp.int32, (BAND_TQ, V7X_LANES), 1)
    first_half = lane < HEAD_DIM
    nt = (((1,), (1,)), ((), ()))
    for p in range(HEADS // 2):
        cs = slice(V7X_LANES * p, V7X_LANES * (p + 1))
        qp = q_ref[:, cs] * (HEAD_DIM ** -0.5)
        starts = []
        for i in range(BAND_NBLK):
            kidx = t - (BAND_NBLK - 1) + i
            starts.append((kidx, pl.multiple_of(jnp.maximum(kidx, 0) * BAND_TQ, BAND_TQ)))
        outs = []
        for e in range(2):
            h = 2 * p + e
            qe = jnp.where(first_half if e == 0 else jnp.logical_not(first_half), qp,
                           jnp.zeros_like(qp))
            s_blocks = []
            for i, (kidx, ks) in enumerate(starts):
                kblk = k_ref[pl.ds(ks, BAND_TQ), cs]
                s = lax.dot_general(qe, kblk, nt, preferred_element_type=F32)
                s = s + bias_ref[h, :, BAND_TQ * i:BAND_TQ * (i + 1)]
                if i < BAND_NBLK - 1:
                    s = jnp.where(kidx >= 0, s, NEG)
                s_blocks.append(s)
            m_el = s_blocks[0]
            for s in s_blocks[1:]:
                m_el = jnp.maximum(m_el, s)
            m = jnp.max(m_el, axis=-1, keepdims=True)
            l_el = None
            acc = None
            for (kidx, ks), s in zip(starts, s_blocks):
                pr = jnp.exp(s - m)
                l_el = pr if l_el is None else l_el + pr
                vblk = v_ref[pl.ds(ks, BAND_TQ), cs]
                pv = jnp.dot(pr.astype(BF16), vblk, preferred_element_type=F32)
                acc = pv if acc is None else acc + pv
            l = jnp.sum(l_el, axis=-1, keepdims=True)
            outs.append(acc * (1.0 / l))
        o_ref[:, cs] = jnp.where(first_half, outs[0], outs[1])


def _band(qkv3, abias):
    b, s, _ = qkv3.shape
    grid = (b, s // BAND_TQ)
    in_specs = [
        pl.BlockSpec((None, BAND_TQ, WIDTH), lambda bi, t: (bi, t, 0)),
        pl.BlockSpec((None, s, WIDTH), lambda bi, t: (bi, 0, 1)),
        pl.BlockSpec((None, s, WIDTH), lambda bi, t: (bi, 0, 2)),
        pl.BlockSpec(abias.shape, lambda bi, t: (0, 0, 0)),
    ]
    out_specs = pl.BlockSpec((None, BAND_TQ, WIDTH), lambda bi, t: (bi, t, 0))
    blk = 2 * (_nbytes((BAND_TQ, WIDTH), BF16) + 2 * _nbytes((s, WIDTH), BF16)
               + _nbytes(abias.shape, F32) + _nbytes((BAND_TQ, WIDTH), F32))
    return pl.pallas_call(
        _band_kernel, out_shape=jax.ShapeDtypeStruct((b, s, WIDTH), F32),
        grid=grid, in_specs=in_specs, out_specs=out_specs, name="band",
        compiler_params=pltpu.CompilerParams(
            dimension_semantics=("arbitrary", "arbitrary"), vmem_limit_bytes=_vmem_limit(blk)),
    )(qkv3, qkv3, qkv3, abias)


def _make_dsa_kernel(n_sel):
    T = DSA_T
    kf = float(n_sel)
    idx_scale = (HEADS ** -0.5) * (HEAD_DIM ** -0.5)

    def kernel(qbT_ref, qiT_ref, wiT_ref, ki_ref, kb_ref, vbT_ref, gbias_ref, adm_ref, o_ref,
               sc_ref, m_ref, l_ref, acc_ref, qs_ref, s_ref, p_ref):
        j = pl.program_id(1)
        nt = j + 1
        zrow = jnp.zeros((1, T), F32)

        def tile_rows(kt):
            return pl.ds(pl.multiple_of(kt * T, T), T)

        wis = wiT_ref[...] * idx_scale

        def p1(kt, carry):
            mn, mx = carry
            ki_t = ki_ref[tile_rows(kt), :]
            acc = jnp.zeros((T, T), F32)
            for h in range(HEADS):
                lg = jnp.dot(ki_t, qiT_ref[HEAD_DIM * h:HEAD_DIM * (h + 1), :],
                             preferred_element_type=F32)
                acc = acc + jnp.maximum(lg, 0.0) * wis[h:h + 1, :]
            mn = jnp.minimum(mn, jnp.min(acc, axis=0, keepdims=True))
            mx = jnp.maximum(mx, jnp.max(acc, axis=0, keepdims=True))
            is_diag = (kt == j).astype(jnp.int32)
            sc_ref[tile_rows(kt), :] = acc + adm_ref[is_diag]
            return mn, mx

        lo0, hi0 = lax.fori_loop(0, nt, p1, (jnp.full((1, T), jnp.inf, F32),
                                             jnp.full((1, T), -jnp.inf, F32)))

        qpos = j * T + lax.broadcasted_iota(jnp.int32, (1, T), 1)
        n_adm = ((qpos // CHUNK + 1) * CHUNK).astype(F32)
        small = n_adm <= kf

        def count_ge(th):
            def body(kt, c8):
                blk = sc_ref[tile_rows(kt), :]
                ind = jnp.where(blk >= th, 1.0, 0.0)
                return c8 + jnp.sum(ind.reshape(T // 8, 8, T), axis=0)
            c8 = lax.fori_loop(0, nt, body, jnp.zeros((8, T), F32))
            return jnp.sum(c8, axis=0, keepdims=True)

        def any_open(done):
            return jnp.max(jnp.where(done, 0.0, 1.0)) > 0.0

        def b_cond(st):
            it, _, _, _, open_ = st
            return jnp.logical_and(it < BISECT_MAX_ITERS, open_)

        def b_body(st):
            it, lo, hi, cl, _ = st
            mid = 0.5 * lo + 0.5 * hi
            c = count_ge(mid)
            ge = c >= kf
            lo = jnp.where(ge, mid, lo)
            cl = jnp.where(ge, c, cl)
            hi = jnp.where(ge, hi, mid)
            return it + 1, lo, hi, cl, any_open(jnp.logical_or(small, cl == kf))

        _, lo, hi, cl, open_ = lax.while_loop(
            b_cond, b_body,
            (jnp.int32(0), lo0, hi0, n_adm, any_open(jnp.logical_or(small, n_adm == kf))))

        def fast_mask():
            thr = jnp.where(small, F32_MIN, lo)

            def body(kt, _):
                blk = sc_ref[tile_rows(kt), :]
                sc_ref[tile_rows(kt), :] = jnp.where(blk >= thr, 0.0, NEG)
                return 0
            lax.fori_loop(0, nt, body, 0)

        def exact_mask():
            def done_of(lo_, hi_, cl_):
                return jnp.logical_or(jnp.logical_or(small, cl_ == kf), lo_ >= hi_)

            def s_body(st):
                lo_, hi_, cl_, _ = st
                done = done_of(lo_, hi_, cl_)
                mid = 0.5 * lo_ + 0.5 * hi_
                mid = jnp.where(mid > lo_, mid, hi_)

                def body(kt, c3):
                    c, mn_ge, mx_lt = c3
                    blk = sc_ref[tile_rows(kt), :]
                    ge_ = blk >= mid
                    c = c + jnp.sum(jnp.where(ge_, 1.0, 0.0), axis=0, keepdims=True)
                    mn_ge = jnp.minimum(mn_ge, jnp.min(jnp.where(ge_, blk, jnp.inf),
                                                       axis=0, keepdims=True))
                    mx_lt = jnp.maximum(mx_lt, jnp.max(jnp.where(ge_, -jnp.inf, blk),
                                                       axis=0, keepdims=True))
                    return c, mn_ge, mx_lt
                c, mn_ge, mx_lt = lax.fori_loop(
                    0, nt, body, (zrow, jnp.full((1, T), jnp.inf, F32),
                                  jnp.full((1, T), -jnp.inf, F32)))
                ge = c >= kf
                lo_n = jnp.where(done, lo_, jnp.where(ge, mn_ge, lo_))
                cl_n = jnp.where(done, cl_, jnp.where(ge, c, cl_))
                hi_n = jnp.where(done, hi_, jnp.where(ge, hi_, mx_lt))
                return lo_n, hi_n, cl_n, any_open(done_of(lo_n, hi_n, cl_n))

            lo_e, _, _, _ = lax.while_loop(
                lambda st: st[3], s_body, (lo, hi, cl, any_open(done_of(lo, hi, cl))))
            thr = jnp.where(small, F32_MIN, lo_e)

            def gt_body(kt, c):
                blk = sc_ref[tile_rows(kt), :]
                return c + jnp.sum(jnp.where(blk > thr, 1.0, 0.0), axis=0, keepdims=True)
            need = kf - lax.fori_loop(0, nt, gt_body, zrow)

            r_io = lax.broadcasted_iota(jnp.int32, (T, T), 0)
            c_io = lax.broadcasted_iota(jnp.int32, (T, T), 1)
            ltri = jnp.where(c_io < r_io, 1.0, 0.0).astype(BF16)

            def m_body(kt, carry):
                blk = sc_ref[tile_rows(kt), :]
                eq = jnp.where(blk == thr, 1.0, 0.0)
                rank = jnp.dot(ltri, eq.astype(BF16), preferred_element_type=F32) + carry
                take = jnp.logical_or(blk > thr,
                                      jnp.logical_and(blk == thr, rank < need))
                sc_ref[tile_rows(kt), :] = jnp.where(take, 0.0, NEG)
                return carry + jnp.sum(eq, axis=0, keepdims=True)
            lax.fori_loop(0, nt, m_body, zrow)

        lax.cond(open_, exact_mask, fast_mask)

        m_ref[...] = jnp.full(m_ref.shape, NEG, F32)
        l_ref[...] = jnp.zeros(l_ref.shape, F32)
        acc_ref[...] = jnp.zeros(acc_ref.shape, F32)
        qs_ref[...] = qbT_ref[...] * (HEAD_DIM ** -0.5)

        def make_p3(near):
            def p3(kt, _):
                kb_t = kb_ref[tile_rows(kt), :]
                vT_t = vbT_ref[kt]
                msk = sc_ref[tile_rows(kt), :]
                tmax = []
                for h in range(HEADS):
                    rows = slice(HEAD_DIM * h, HEAD_DIM * (h + 1))
                    add = msk + gbias_ref[kt - j + 1, h] if near else msk
                    s = jnp.dot(kb_t, qs_ref[rows, :], preferred_element_type=F32) + add
                    s_ref[h] = s
                    tmax.append(jnp.max(s, axis=0, keepdims=True))
                m_old = m_ref[...]
                m_new = jnp.maximum(m_old, jnp.concatenate(tmax, axis=0))
                alpha = jnp.exp(m_old - m_new)
                m_ref[...] = m_new
                lsum = []
                for h in range(HEADS):
                    pr = jnp.exp(s_ref[h] - m_new[h:h + 1, :])
                    lsum.append(jnp.sum(pr, axis=0, keepdims=True))
                    p_ref[h] = pr.astype(BF16)
                l_ref[...] = alpha * l_ref[...] + jnp.concatenate(lsum, axis=0)
                for h in range(HEADS):
                    rows = slice(HEAD_DIM * h, HEAD_DIM * (h + 1))
                    acc_ref[rows, :] = alpha[h:h + 1, :] * acc_ref[rows, :] + jnp.dot(
                        vT_t, p_ref[h], preferred_element_type=F32)
                return 0
            return p3

        n_far = jnp.maximum(j - 1, 0)
        lax.fori_loop(0, n_far, make_p3(False), 0)
        lax.fori_loop(n_far, nt, make_p3(True), 0)

        inv = 1.0 / l_ref[...]
        for h in range(HEADS):
            rows = slice(HEAD_DIM * h, HEAD_DIM * (h + 1))
            acc_ref[rows, :] = acc_ref[rows, :] * inv[h:h + 1, :]
        o_ref[...] = acc_ref[...].T

    return kernel


def _dsa(qbT, qiT, wiT, ki3, kb3, vbT4, gbias, adm, n_sel):
    b, s, _ = ki3.shape
    T = DSA_T
    nq = s // T
    grid = (b, nq)
    in_specs = [
        pl.BlockSpec((WIDTH, T), lambda bi, j: (0, bi * nq + j)),
        pl.BlockSpec((WIDTH, T), lambda bi, j: (0, bi * nq + j)),
        pl.BlockSpec((HEADS, T), lambda bi, j: (0, bi * nq + j)),
        pl.BlockSpec((None, s, HEAD_DIM), lambda bi, j: (bi, 0, 0)),
        pl.BlockSpec((None, s, HEAD_DIM), lambda bi, j: (bi, 0, 0)),
        pl.BlockSpec((None, nq, HEAD_DIM, T), lambda bi, j: (bi, 0, 0, 0)),
        pl.BlockSpec(gbias.shape, lambda bi, j: (0, 0, 0, 0)),
        pl.BlockSpec(adm.shape, lambda bi, j: (0, 0, 0)),
    ]
    out_specs = pl.BlockSpec((None, T, WIDTH), lambda bi, j: (bi, j, 0))
    scratch = [
        pltpu.VMEM((s, T), F32),
        pltpu.VMEM((HEADS, T), F32),
        pltpu.VMEM((HEADS, T), F32),
        pltpu.VMEM((WIDTH, T), F32),
        pltpu.VMEM((WIDTH, T), BF16),
        pltpu.VMEM((HEADS, T, T), F32),
        pltpu.VMEM((HEADS, T, T), BF16),
    ]
    blk = (2 * (2 * _nbytes((WIDTH, T), BF16) + _nbytes((HEADS, T), F32)
                + 2 * _nbytes((s, V7X_LANES), BF16) + _nbytes((HEAD_DIM, s), BF16)
                + _nbytes(gbias.shape, F32) + _nbytes(adm.shape, F32)
                + _nbytes((T, WIDTH), F32))
           + _nbytes((s, T), F32) + 2 * _nbytes((HEADS, T), F32) + _nbytes((WIDTH, T), F32)
           + _nbytes((WIDTH, T), BF16) + _nbytes((HEADS, T, T), F32)
           + _nbytes((HEADS, T, T), BF16))
    return pl.pallas_call(
        _make_dsa_kernel(n_sel), out_shape=jax.ShapeDtypeStruct((b, s, WIDTH), F32),
        grid=grid, in_specs=in_specs, out_specs=out_specs, scratch_shapes=scratch, name="dsa",
        compiler_params=pltpu.CompilerParams(
            dimension_semantics=("arbitrary", "arbitrary"), vmem_limit_bytes=_vmem_limit(blk)),
    )(qbT, qiT, wiT, ki3, kb3, vbT4, gbias, adm)


def _merge_kernel(x_ref, ya_ref, yb_ref, gate_ref, wa_ref, wb_ref, wo_ref, fg_ref, o_ref):
    d = x_ref.shape[1]
    za = gate_ref[:, 0:WIDTH]
    zb = gate_ref[:, WIDTH:2 * WIDTH]
    ga = gate_ref[:, 2 * WIDTH:2 * WIDTH + d]
    gb = gate_ref[:, 2 * WIDTH + d:2 * WIDTH + 2 * d]
    ua = (ya_ref[...] * (za * _sigmoid(za))).astype(BF16)
    ub = (yb_ref[...] * (zb * _sigmoid(zb))).astype(BF16)
    pa = jnp.dot(ua, wa_ref[...], preferred_element_type=F32)
    pb = jnp.dot(ub, wb_ref[...], preferred_element_type=F32)
    merged = _sigmoid(ga) * pa + _sigmoid(gb) * pb
    h = x_ref[...] + jnp.dot(merged.astype(BF16), wo_ref[...], preferred_element_type=F32)
    ms = jnp.mean(h * h, axis=-1, keepdims=True)
    o_ref[...] = (h * lax.rsqrt(ms + EPS)) * fg_ref[...]


def _merge(x2, ya2, yb2, gates, wa, wb, wo, fg, tm):
    n, d = x2.shape
    n_gate = gates.shape[1]
    grid = (n // tm,)
    row = lambda i: (i, 0)
    const = lambda i: (0, 0)
    in_specs = [
        pl.BlockSpec((tm, d), row),
        pl.BlockSpec((tm, WIDTH), row),
        pl.BlockSpec((tm, WIDTH), row),
        pl.BlockSpec((tm, n_gate), row),
        pl.BlockSpec(wa.shape, const),
        pl.BlockSpec(wb.shape, const),
        pl.BlockSpec(wo.shape, const),
        pl.BlockSpec((1, d), const),
    ]
    blk = 2 * (2 * _nbytes((tm, d), F32) + 2 * _nbytes((tm, WIDTH), F32)
               + _nbytes((tm, n_gate), F32) + _nbytes(wa.shape, BF16) + _nbytes(wb.shape, BF16)
               + _nbytes(wo.shape, BF16))
    return pl.pallas_call(
        _merge_kernel, out_shape=jax.ShapeDtypeStruct((n, d), F32),
        grid=grid, in_specs=in_specs, out_specs=pl.BlockSpec((tm, d), row), name="merge",
        compiler_params=pltpu.CompilerParams(
            dimension_semantics=("arbitrary",), vmem_limit_bytes=_vmem_limit(blk)),
    )(x2, ya2, yb2, gates, wa, wb, wo, fg)


def _toeplitz(g, nrows, ncols):
    period = nrows + ncols
    lead = g.shape[:-1]
    p = jnp.concatenate([g[..., ::-1], jnp.zeros(lead + (1,), g.dtype)], axis=-1)
    flat = jnp.tile(p, (1,) * len(lead) + (nrows,))[..., :nrows * (period - 1)]
    x = flat.reshape(lead + (nrows, period - 1))
    return x[..., nrows - 1:nrows - 1 + ncols]


def _band_bias(a_rel_bias):
    pad = A_LEFT_CHUNKS * CHUNK
    rel = np.arange(BAND_TQ + BAND_WIN - 1) - (BAND_WIN - 1) + pad
    idx = np.clip(rel, -A_REL_CLIP, A_REL_CLIP) + A_REL_CLIP
    lo, hi = int(idx[0]), int(idx.max())
    n_flat = int((idx == hi).sum()) - 1
    assert np.array_equal(idx, np.minimum(np.arange(lo, lo + idx.size), hi))
    ab = a_rel_bias.astype(F32)
    g = jnp.concatenate([ab[:, lo:hi + 1], jnp.broadcast_to(ab[:, hi:hi + 1], (HEADS, n_flat))],
                        axis=1)
    bias = _toeplitz(g, BAND_TQ, BAND_WIN)
    qq = np.arange(BAND_TQ)[:, None]
    jj = np.arange(BAND_WIN)[None, :]
    in_band = np.logical_and(jj // CHUNK >= qq // CHUNK, jj // CHUNK <= qq // CHUNK + A_LEFT_CHUNKS)
    return jnp.where(jnp.asarray(in_band)[None], bias, NEG)


def _dsa_bias(t5_bias):
    T = DSA_T
    far = T5_BUCKETS // 2 - 1
    tb = t5_bias.astype(F32)
    slabs = []
    for off in (-T, 0):
        rel = jnp.arange(2 * T - 1, dtype=jnp.int32) - (T - 1) + off
        onehot = (_t5_bucket(rel)[:, None] == jnp.arange(T5_BUCKETS)[None, :]).astype(F32)
        g = jnp.dot(onehot, tb - tb[far], precision=lax.Precision.HIGHEST)
        slabs.append(_toeplitz(g.T, T, T))
    return jnp.stack(slabs)


def _far_bucket_is_constant(s):
    half = T5_BUCKETS // 2
    max_exact = half // 2
    for n in (DSA_T + 1, max(s - 1, DSA_T + 1)):
        large = max_exact + int(math.log(n / max_exact) / math.log(T5_MAX_DIST / max_exact)
                                * (half - max_exact))
        if min(large, half - 1) != half - 1:
            return False
    return True


def kernel(x, norm_gain, w_in, a_rel_bias, t5_bias, w_a_out, w_b_out, w_out, final_gain):
    b, s, d = x.shape
    depth = w_in.shape[0]
    n = b * s
    assert s % DSA_T == 0 and s % BAND_TQ == 0
    assert _far_bucket_is_constant(s)
    tm = 512 if n % 512 == 0 else DSA_T
    n_sel = min(TOPK_MAX, s // 4)

    splits = (WIDTH,) * 4 + (WIDTH, HEAD_DIM, HEAD_DIM, WIDTH) + (WIDTH, HEAD_DIM, HEADS) + (d, d)
    offs = np.concatenate([[0], np.cumsum(splits)])
    assert offs[-1] == w_in.shape[2]

    abias = _band_bias
    gbias = _dsa_bias(t5_bias)
    kk = jnp.arange(DSA_T)[:, None] // CHUNK
    qq = jnp.arange(DSA_T)[None, :] // CHUNK
    adm = jnp.stack([jnp.zeros((DSA_T, DSA_T), F32),
                     jnp.where(kk <= qq, 0.0, -jnp.inf).astype(F32)])

    h2 = x.reshape(n, d)
    for l in range(depth):
        w = w_in[l]
        (qa, ka, va, za, qb, kb, vb, zb, qi, ki, wi, ga, gb) = [
            w[:, offs[i]:offs[i + 1]] for i in range(len(splits))]
        wn = jnp.concatenate([qa, ka, va, ki, kb], axis=1).astype(BF16)
        wg = jnp.concatenate([za, zb, ga, gb], axis=1).astype(BF16)
        wt = jnp.concatenate([qb, qi, vb, wi, jnp.zeros((d, HEADS), w.dtype)],
                             axis=1).T.astype(BF16)

        qkv, ki_o, kb_o, gates, qbT, qiT, vbT, wiT = _proj(
            h2, norm_gain[l].reshape(1, d).astype(F32), wn, wg, wt, tm)

        ya = _band(qkv.reshape(b, s, 3 * WIDTH), abias(a_rel_bias[l]))
        yb = _dsa(qbT, qiT, wiT, ki_o.reshape(b, s, HEAD_DIM), kb_o.reshape(b, s, HEAD_DIM),
                  vbT.reshape(b, s // DSA_T, HEAD_DIM, DSA_T), gbias, adm, n_sel)

        assert depth == 1
        h2 = _merge(h2, ya.reshape(n, WIDTH), yb.reshape(n, WIDTH), gates,
                    w_a_out[l].astype(BF16), w_b_out[l].astype(BF16), w_out[l].astype(BF16),
                    final_gain.reshape(1, d).astype(F32), tm)
    return h2.reshape(b, s, d)
```

```python
import math

import numpy as np
import jax
import jax.numpy as jnp
from jax import lax
from jax.experimental import pallas as pl
from jax.experimental.pallas import tpu as pltpu

F32 = jnp.float32
BF16 = jnp.bfloat16

CHUNK = 64
EPS = 1e-6
HEADS = 8
HEAD_DIM = 64
WIDTH = HEADS * HEAD_DIM
A_LEFT_CHUNKS = 8
A_REL_CLIP = 256
TOPK_MAX = 256
T5_BUCKETS = 32
T5_MAX_DIST = 128
NEG = -1e30
F32_MIN = float(np.finfo(np.float32).min)
LOG2E = math.log2(math.e)

V7X_VMEM_BYTES = 64 * 1024 * 1024
V7X_LANES = 128
VMEM_CAP_BYTES = 60000 * 1024
COMPILER_TEMP_BYTES = 8 * 1024 * 1024

BAND_TQ = 2 * CHUNK
BAND_WIN = (A_LEFT_CHUNKS + 2) * CHUNK
BAND_NBLK = BAND_WIN // BAND_TQ
DSA_T = 256
BISECT_MAX_ITERS = 26


def _vmem_limit(block_bytes):
    return int(min(block_bytes + COMPILER_TEMP_BYTES, VMEM_CAP_BYTES))


def _nbytes(shape, dtype):
    return int(np.prod(shape)) * jnp.dtype(dtype).itemsize


def _t5_bucket(rel):
    half = T5_BUCKETS // 2
    max_exact = half // 2
    ret = jnp.where(rel > 0, half, 0)
    n = jnp.abs(rel)
    nf = jnp.maximum(n, 1).astype(F32)
    large = max_exact + (jnp.log(nf / max_exact) / math.log(T5_MAX_DIST / max_exact)
                         * (half - max_exact)).astype(jnp.int32)
    large = jnp.minimum(large, half - 1)
    return ret + jnp.where(n < max_exact, n, large)


def _sigmoid(v):
    return 1.0 / (1.0 + jnp.exp(-v))


def _proj_kernel(x_ref, g_ref, wn_ref, wg_ref, wt_ref,
                 qkv_ref, ki_ref, kb_ref, gate_ref, qbT_ref, qiT_ref, vbT_ref, wiT_ref):
    xf = x_ref[...]
    ms = jnp.mean(xf * xf, axis=-1, keepdims=True)
    hn = ((xf * lax.rsqrt(ms + EPS)) * g_ref[...]).astype(BF16)

    n_qkv = qkv_ref.shape[1]
    for c0 in range(0, n_qkv, WIDTH):
        qkv_ref[:, c0:c0 + WIDTH] = jnp.dot(
            hn, wn_ref[:, c0:c0 + WIDTH], preferred_element_type=F32).astype(BF16)
    kk = jnp.dot(hn, wn_ref[:, n_qkv:n_qkv + 2 * HEAD_DIM], preferred_element_type=F32)
    ki_ref[...] = kk[:, :HEAD_DIM].astype(BF16)
    kb_ref[...] = kk[:, HEAD_DIM:].astype(BF16)

    n_gate = gate_ref.shape[1]
    for c0 in range(0, n_gate, WIDTH):
        gate_ref[:, c0:c0 + WIDTH] = jnp.dot(
            hn, wg_ref[:, c0:c0 + WIDTH], preferred_element_type=F32)

    nt = (((1,), (1,)), ((), ()))
    qbT_ref[...] = lax.dot_general(wt_ref[0:WIDTH, :], hn, nt,
                                   preferred_element_type=F32).astype(BF16)
    qiT_ref[...] = lax.dot_general(wt_ref[WIDTH:2 * WIDTH, :], hn, nt,
                                   preferred_element_type=F32).astype(BF16)
    tail = lax.dot_general(wt_ref[2 * WIDTH:, :], hn, nt, preferred_element_type=F32)
    n_sub = vbT_ref.shape[0]
    for s in range(n_sub):
        vbT_ref[s] = tail[0:HEAD_DIM, s * DSA_T:(s + 1) * DSA_T].astype(BF16)
    wiT_ref[...] = tail[HEAD_DIM:HEAD_DIM + HEADS, :]


def _proj(x2, gain, wn, wg, wt, tm):
    n, d = x2.shape
    n_qkv = 3 * WIDTH
    n_gate = wg.shape[1]
    n_sub = tm // DSA_T
    grid = (n // tm,)
    const = lambda i: (0, 0)
    in_specs = [
        pl.BlockSpec((tm, d), lambda i: (i, 0)),
        pl.BlockSpec((1, d), const),
        pl.BlockSpec(wn.shape, const),
        pl.BlockSpec(wg.shape, const),
        pl.BlockSpec(wt.shape, const),
    ]
    out_shape = (
        jax.ShapeDtypeStruct((n, n_qkv), BF16),
        jax.ShapeDtypeStruct((n, HEAD_DIM), BF16),
        jax.ShapeDtypeStruct((n, HEAD_DIM), BF16),
        jax.ShapeDtypeStruct((n, n_gate), F32),
        jax.ShapeDtypeStruct((WIDTH, n), BF16),
        jax.ShapeDtypeStruct((WIDTH, n), BF16),
        jax.ShapeDtypeStruct((n // DSA_T, HEAD_DIM, DSA_T), BF16),
        jax.ShapeDtypeStruct((HEADS, n), F32),
    )
    out_specs = (
        pl.BlockSpec((tm, n_qkv), lambda i: (i, 0)),
        pl.BlockSpec((tm, HEAD_DIM), lambda i: (i, 0)),
        pl.BlockSpec((tm, HEAD_DIM), lambda i: (i, 0)),
        pl.BlockSpec((tm, n_gate), lambda i: (i, 0)),
        pl.BlockSpec((WIDTH, tm), lambda i: (0, i)),
        pl.BlockSpec((WIDTH, tm), lambda i: (0, i)),
        pl.BlockSpec((n_sub, HEAD_DIM, DSA_T), lambda i: (i, 0, 0)),
        pl.BlockSpec((HEADS, tm), lambda i: (0, i)),
    )
    blk = 2 * (_nbytes((tm, d), F32) + _nbytes(wn.shape, BF16) + _nbytes(wg.shape, BF16)
               + _nbytes(wt.shape, BF16) + _nbytes((tm, n_qkv), BF16)
               + 2 * _nbytes((tm, V7X_LANES), BF16) + _nbytes((tm, n_gate), F32)
               + 2 * _nbytes((WIDTH, tm), BF16) + _nbytes((HEAD_DIM, tm), BF16)
               + _nbytes((HEADS, tm), F32))
    return pl.pallas_call(
        _proj_kernel, out_shape=out_shape, grid=grid, in_specs=in_specs, out_specs=out_specs,
        name="proj",
        compiler_params=pltpu.CompilerParams(
            dimension_semantics=("arbitrary",), vmem_limit_bytes=_vmem_limit(blk)),
    )(x2, gain, wn, wg, wt)


def _band_kernel(q_ref, k_ref, v_ref, bias_ref, o_ref):
    t = pl.program_id(1)
    lane = lax.broadcasted_iota(jnp.int32, (BAND_TQ, V7X_LANES), 1)
    first_half = lane < HEAD_DIM
    nt = (((1,), (1,)), ((), ()))
    for p in range(HEADS // 2):
        cs = slice(V7X_LANES * p, V7X_LANES * (p + 1))
        qp = q_ref[:, cs] * (HEAD_DIM ** -0.5)
        starts = []
        for i in range(BAND_NBLK):
            kidx = t - (BAND_NBLK - 1) + i
            starts.append((kidx, pl.multiple_of(jnp.maximum(kidx, 0) * BAND_TQ, BAND_TQ)))
        outs = []
        for e in range(2):
            h = 2 * p + e
            qe = jnp.where(first_half if e == 0 else jnp.logical_not(first_half), qp,
                           jnp.zeros_like(qp))
            s_blocks = []
            for i, (kidx, ks) in enumerate(starts):
                kblk = k_ref[pl.ds(ks, BAND_TQ), cs]
                s = lax.dot_general(qe, kblk, nt, preferred_element_type=F32)
                s = s + bias_ref[h, :, BAND_TQ * i:BAND_TQ * (i + 1)]
                if i < BAND_NBLK - 1:
                    s = jnp.where(kidx >= 0, s, NEG)
                s_blocks.append(s)
            m_el = s_blocks[0]
            for s in s_blocks[1:]:
                m_el = jnp.maximum(m_el, s)
            m = jnp.max(m_el, axis=-1, keepdims=True)
            l_el = None
            acc = None
            for (kidx, ks), s in zip(starts, s_blocks):
                pr = jnp.exp(s - m)
                l_el = pr if l_el is None else l_el + pr
                vblk = v_ref[pl.ds(ks, BAND_TQ), cs]
                pv = jnp.dot(pr.astype(BF16), vblk, preferred_element_type=F32)
                acc = pv if acc is None else acc + pv
            l = jnp.sum(l_el, axis=-1, keepdims=True)
            outs.append(acc * (1.0 / l))
        o_ref[:, cs] = jnp.where(first_half, outs[0], outs[1])


def _band(qkv3, abias):
    b, s, _ = qkv3.shape
    grid = (b, s // BAND_TQ)
    in_specs = [
        pl.BlockSpec((None, BAND_TQ, WIDTH), lambda bi, t: (bi, t, 0)),
        pl.BlockSpec((None, s, WIDTH), lambda bi, t: (bi, 0, 1)),
        pl.BlockSpec((None, s, WIDTH), lambda bi, t: (bi, 0, 2)),
        pl.BlockSpec(abias.shape, lambda bi, t: (0, 0, 0)),
    ]
    out_specs = pl.BlockSpec((None, BAND_TQ, WIDTH), lambda bi, t: (bi, t, 0))
    blk = 2 * (_nbytes((BAND_TQ, WIDTH), BF16) + 2 * _nbytes((s, WIDTH), BF16)
               + _nbytes(abias.shape, F32) + _nbytes((BAND_TQ, WIDTH), F32))
    return pl.pallas_call(
        _band_kernel, out_shape=jax.ShapeDtypeStruct((b, s, WIDTH), F32),
        grid=grid, in_specs=in_specs, out_specs=out_specs, name="band",
        compiler_params=pltpu.CompilerParams(
            dimension_semantics=("arbitrary", "arbitrary"), vmem_limit_bytes=_vmem_limit(blk)),
    )(qkv3, qkv3, qkv3, abias)


def _make_dsa_kernel(n_sel):
    T = DSA_T
    kf = float(n_sel)
    idx_scale = (HEADS ** -0.5) * (HEAD_DIM ** -0.5)

    def kernel(qbT_ref, qiT_ref, wiT_ref, ki_ref, kb_ref, vbT_ref, gbias_ref, adm_ref, o_ref,
               sc_ref, m_ref, l_ref, acc_ref, tmax_ref, s_ref, p_ref):
        j = pl.program_id(1)
        nt = j + 1
        zrow = jnp.zeros((1, T), F32)

        def tile_rows(kt):
            return pl.ds(pl.multiple_of(kt * T, T), T)

        wis = wiT_ref[...] * idx_scale

        def fold8(a):
            return jnp.sum(a.reshape(a.shape[0] // 8, 8, T), axis=0)

        HT = T // 2

        def p1(kt, carry):
            mn, mx, cgt0, cge0 = carry
            is_diag = (kt == j).astype(jnp.int32)
            for half in range(2):
                rows = pl.ds(pl.multiple_of(kt * T + half * HT, HT), HT)
                ki_t = ki_ref[rows, :]
                acc = jnp.zeros((HT, T), F32)
                for h in range(HEADS):
                    lg = jnp.dot(ki_t, qiT_ref[HEAD_DIM * h:HEAD_DIM * (h + 1), :],
                                 preferred_element_type=F32)
                    acc = acc + jnp.maximum(lg, 0.0) * wis[h:h + 1, :]
                mn = jnp.minimum(mn, jnp.min(acc, axis=0, keepdims=True))
                mx = jnp.maximum(mx, jnp.max(acc, axis=0, keepdims=True))
                sc = acc + adm_ref[is_diag, half * HT:(half + 1) * HT, :]
                sc_ref[rows, :] = sc
                cgt0 = cgt0 + fold8(jnp.where(sc > 0.0, 1.0, 0.0))
                cge0 = cge0 + fold8(jnp.where(sc >= 0.0, 1.0, 0.0))
            return mn, mx, cgt0, cge0

        z8 = jnp.zeros((8, T), F32)
        lo0, hi0, cgt0, cge0 = lax.fori_loop(
            0, nt, p1, (jnp.full((1, T), jnp.inf, F32), jnp.full((1, T), -jnp.inf, F32), z8, z8))
        cgt0 = jnp.sum(cgt0, axis=0, keepdims=True)
        cge0 = jnp.sum(cge0, axis=0, keepdims=True)

        qpos = j * T + lax.broadcasted_iota(jnp.int32, (1, T), 1)
        n_adm = ((qpos // CHUNK + 1) * CHUNK).astype(F32)
        small = n_adm <= kf
        above0 = cgt0 >= kf
        below0 = cge0 < kf
        zero_tie = jnp.logical_not(jnp.logical_or(above0, below0))
        lo_pos = lo0 > 0.0
        lo1 = jnp.where(zero_tie, 0.0, jnp.where(jnp.logical_and(above0, jnp.logical_not(lo_pos)),
                                                 0.0, lo0))
        cl1 = jnp.where(jnp.logical_or(zero_tie, jnp.logical_and(above0, jnp.logical_not(lo_pos))),
                        cge0, n_adm)
        hi1 = jnp.where(zero_tie, 0.0, jnp.where(below0, jnp.minimum(hi0, 0.0), hi0))

        @pl.when(nt % 2 == 1)
        def _():
            sc_ref[tile_rows(nt), :] = jnp.full((T, T), -jnp.inf, F32)

        def count_ge(th):
            def body(i, c8):
                for u in range(2):
                    blk = sc_ref[tile_rows(2 * i + u), :]
                    c8 = c8 + fold8(jnp.where(blk >= th, 1.0, 0.0))
                return c8
            return jnp.sum(lax.fori_loop(0, (nt + 1) // 2, body, z8), axis=0, keepdims=True)

        def any_open(done):
            return jnp.max(jnp.where(done, 0.0, 1.0)) > 0.0

        def settled(cl_):
            return jnp.logical_or(jnp.logical_or(small, zero_tie), cl_ == kf)

        def b_cond(st):
            it, _, _, _, open_ = st
            return jnp.logical_and(it < BISECT_MAX_ITERS, open_)

        def b_body(st):
            it, lo, hi, cl, _ = st
            mid = 0.5 * lo + 0.5 * hi
            c = count_ge(mid)
            ge = c >= kf
            lo = jnp.where(ge, mid, lo)
            cl = jnp.where(ge, c, cl)
            hi = jnp.where(ge, hi, mid)
            return it + 1, lo, hi, cl, any_open(settled(cl))

        _, lo, hi, cl, open_ = lax.while_loop(
            b_cond, b_body, (jnp.int32(0), lo1, hi1, cl1, any_open(settled(cl1))))
        has_ties = jnp.logical_or(open_, jnp.max(jnp.where(
            jnp.logical_and(zero_tie, jnp.logical_not(small)), 1.0, 0.0)) > 0.0)

        def fast_mask():
            thr = jnp.where(small, F32_MIN, lo)

            def body(kt, _):
                blk = sc_ref[tile_rows(kt), :]
                sc_ref[tile_rows(kt), :] = jnp.where(blk >= thr, 0.0, NEG)
                return 0
            lax.fori_loop(0, nt, body, 0)

        def exact_mask():
            def done_of(lo_, hi_, cl_):
                return jnp.logical_or(jnp.logical_or(small, cl_ == kf), lo_ >= hi_)

            def s_body(st):
                lo_, hi_, cl_, _ = st
                done = done_of(lo_, hi_, cl_)
                mid = 0.5 * lo_ + 0.5 * hi_
                mid = jnp.where(mid > lo_, mid, hi_)

                def body(kt, c3):
                    c, mn_ge, mx_lt = c3
                    blk = sc_ref[tile_rows(kt), :]
                    ge_ = blk >= mid
                    c = c + jnp.sum(jnp.where(ge_, 1.0, 0.0), axis=0, keepdims=True)
                    mn_ge = jnp.minimum(mn_ge, jnp.min(jnp.where(ge_, blk, jnp.inf),
                                                       axis=0, keepdims=True))
                    mx_lt = jnp.maximum(mx_lt, jnp.max(jnp.where(ge_, -jnp.inf, blk),
                                                       axis=0, keepdims=True))
                    return c, mn_ge, mx_lt
                c, mn_ge, mx_lt = lax.fori_loop(
                    0, nt, body, (zrow, jnp.full((1, T), jnp.inf, F32),
                                  jnp.full((1, T), -jnp.inf, F32)))
                ge = c >= kf
                lo_n = jnp.where(done, lo_, jnp.where(ge, mn_ge, lo_))
                cl_n = jnp.where(done, cl_, jnp.where(ge, c, cl_))
                hi_n = jnp.where(done, hi_, jnp.where(ge, hi_, mx_lt))
                return lo_n, hi_n, cl_n, any_open(done_of(lo_n, hi_n, cl_n))

            lo_e, _, cl_e, _ = lax.while_loop(
                lambda st: st[3], s_body, (lo, hi, cl, any_open(done_of(lo, hi, cl))))
            thr = jnp.where(small, F32_MIN, lo_e)

            def count_gt():
                def body(kt, c8):
                    blk = sc_ref[tile_rows(kt), :]
                    return c8 + fold8(jnp.where(blk > thr, 1.0, 0.0))
                return jnp.sum(lax.fori_loop(0, nt, body, z8), axis=0, keepdims=True)
            cgt = lax.cond(open_, count_gt, lambda: cgt0)
            need = jnp.where(jnp.logical_or(small, cl_e == kf), jnp.inf, kf - cgt)

            UNDECIDED = -1.0

            def m_body(kt, st):
                carry, xt, xc = st
                blk = sc_ref[tile_rows(kt), :]
                eq = jnp.where(blk == thr, 1.0, 0.0)
                cnt = jnp.sum(fold8(eq), axis=0, keepdims=True)
                has_budget = carry < need
                crossing = jnp.logical_and(has_budget, carry + cnt > need)
                tie_val = jnp.where(crossing, UNDECIDED, jnp.where(has_budget, 0.0, NEG))
                sc_ref[tile_rows(kt), :] = jnp.where(
                    blk > thr, 0.0, jnp.where(blk == thr, tie_val, NEG))
                xt = jnp.where(crossing, kt.astype(F32), xt)
                xc = jnp.where(crossing, carry, xc)
                return carry + cnt, xt, xc
            _, xt, xc = lax.fori_loop(0, nt, m_body, (zrow, jnp.full((1, T), -1.0, F32), zrow))

            def r_body(st):
                xt_, ktf = st
                kt = ktf.astype(jnp.int32)
                msk = sc_ref[tile_rows(kt), :]
                und = msk == UNDECIDED
                r_io = lax.broadcasted_iota(jnp.int32, (T, T), 0)
                c_io = lax.broadcasted_iota(jnp.int32, (T, T), 1)
                ltri = jnp.where(c_io < r_io, 1.0, 0.0).astype(BF16)
                rank = jnp.dot(ltri, jnp.where(und, 1.0, 0.0).astype(BF16),
                               preferred_element_type=F32) + xc
                sc_ref[tile_rows(kt), :] = jnp.where(
                    und, jnp.where(rank < need, 0.0, NEG), msk)
                xt_ = jnp.where(xt_ == ktf, -1.0, xt_)
                return xt_, jnp.max(xt_)
            lax.while_loop(lambda st: st[1] >= 0.0, r_body, (xt, jnp.max(xt)))

        lax.cond(has_ties, exact_mask, fast_mask)

        m_ref[...] = jnp.full(m_ref.shape, NEG, F32)
        l_ref[...] = jnp.zeros(l_ref.shape, F32)
        acc_ref[...] = jnp.zeros(acc_ref.shape, F32)

        def step(kt_a, bias_slab, kt_bc):
            if kt_bc is not None:
                vT_t = vbT_ref[kt_bc]
                m_old = m_ref[...]
                m_new = jnp.maximum(m_old, tmax_ref[...])
                alpha = jnp.exp2(m_old - m_new)
                m_ref[...] = m_new
            if kt_a is not None:
                kb_t = kb_ref[tile_rows(kt_a), :]
                msk = sc_ref[tile_rows(kt_a), :]
            lsum, tmax = [], []
            for h in range(HEADS):
                rows = slice(HEAD_DIM * h, HEAD_DIM * (h + 1))
                if kt_bc is not None:
                    pr = jnp.exp2(s_ref[h] - m_new[h:h + 1, :])
                    lsum.append(jnp.sum(pr, axis=0, keepdims=True))
                    p_ref[h] = pr.astype(BF16)
                if kt_a is not None:
                    add = msk if bias_slab is None else msk + gbias_ref[bias_slab, h]
                    s = jnp.dot(kb_t, qbT_ref[rows, :], preferred_element_type=F32) + add
                    s_ref[h] = s
                    tmax.append(jnp.max(s, axis=0, keepdims=True))
                if kt_bc is not None:
                    acc_ref[rows, :] = alpha[h:h + 1, :] * acc_ref[rows, :] + jnp.dot(
                        vT_t, p_ref[h], preferred_element_type=F32)
            if kt_bc is not None:
                l_ref[...] = alpha * l_ref[...] + jnp.concatenate(lsum, axis=0)
            if kt_a is not None:
                tmax_ref[...] = jnp.concatenate(tmax, axis=0)

        n_far = jnp.maximum(j - 1, 0)
        step(0, jnp.clip(2 - j, 0, 2), None)

        def far_step(a, _):
            step(a, None, a - 1)
            return 0

        def near_step(a, _):
            step(a, a - j + 2, a - 1)
            return 0

        lax.fori_loop(1, n_far, far_step, 0)
        lax.fori_loop(jnp.maximum(n_far, 1), nt, near_step, 0)
        step(None, None, nt - 1)

        inv = 1.0 / l_ref[...]
        for h in range(HEADS):
            rows = slice(HEAD_DIM * h, HEAD_DIM * (h + 1))
            acc_ref[rows, :] = acc_ref[rows, :] * inv[h:h + 1, :]
        o_ref[...] = acc_ref[...].T

    return kernel


def _dsa(qbT, qiT, wiT, ki3, kb3, vbT4, gbias, adm, n_sel):
    b, s, _ = ki3.shape
    T = DSA_T
    nq = s // T
    grid = (b, nq)
    in_specs = [
        pl.BlockSpec((WIDTH, T), lambda bi, j: (0, bi * nq + j)),
        pl.BlockSpec((WIDTH, T), lambda bi, j: (0, bi * nq + j)),
        pl.BlockSpec((HEADS, T), lambda bi, j: (0, bi * nq + j)),
        pl.BlockSpec((None, s, HEAD_DIM), lambda bi, j: (bi, 0, 0)),
        pl.BlockSpec((None, s, HEAD_DIM), lambda bi, j: (bi, 0, 0)),
        pl.BlockSpec((None, nq, HEAD_DIM, T), lambda bi, j: (bi, 0, 0, 0)),
        pl.BlockSpec(gbias.shape, lambda bi, j: (0, 0, 0, 0)),
        pl.BlockSpec(adm.shape, lambda bi, j: (0, 0, 0)),
    ]
    out_specs = pl.BlockSpec((None, T, WIDTH), lambda bi, j: (bi, j, 0))
    scratch = [
        pltpu.VMEM((s + T, T), F32),
        pltpu.VMEM((HEADS, T), F32),
        pltpu.VMEM((HEADS, T), F32),
        pltpu.VMEM((WIDTH, T), F32),
        pltpu.VMEM((HEADS, T), F32),
        pltpu.VMEM((HEADS, T, T), F32),
        pltpu.VMEM((HEADS, T, T), BF16),
    ]
    blk = (2 * (2 * _nbytes((WIDTH, T), BF16) + _nbytes((HEADS, T), F32)
                + 2 * _nbytes((s, V7X_LANES), BF16) + _nbytes((HEAD_DIM, s), BF16)
                + _nbytes(gbias.shape, F32) + _nbytes(adm.shape, F32)
                + _nbytes((T, WIDTH), F32))
           + _nbytes((s + T, T), F32) + 2 * _nbytes((HEADS, T), F32) + _nbytes((WIDTH, T), F32)
           + _nbytes((HEADS, T), F32) + _nbytes((HEADS, T, T), F32)
           + _nbytes((HEADS, T, T), BF16))
    return pl.pallas_call(
        _make_dsa_kernel(n_sel), out_shape=jax.ShapeDtypeStruct((b, s, WIDTH), F32),
        grid=grid, in_specs=in_specs, out_specs=out_specs, scratch_shapes=scratch, name="dsa",
        compiler_params=pltpu.CompilerParams(
            dimension_semantics=("arbitrary", "arbitrary"), vmem_limit_bytes=_vmem_limit(blk)),
    )(qbT, qiT, wiT, ki3, kb3, vbT4, gbias, adm)


def _merge_kernel(x_ref, ya_ref, yb_ref, gate_ref, wa_ref, wb_ref, wo_ref, fg_ref, o_ref):
    d = x_ref.shape[1]
    za = gate_ref[:, 0:WIDTH]
    zb = gate_ref[:, WIDTH:2 * WIDTH]
    ga = gate_ref[:, 2 * WIDTH:2 * WIDTH + d]
    gb = gate_ref[:, 2 * WIDTH + d:2 * WIDTH + 2 * d]
    ua = (ya_ref[...] * (za * _sigmoid(za))).astype(BF16)
    ub = (yb_ref[...] * (zb * _sigmoid(zb))).astype(BF16)
    pa = jnp.dot(ua, wa_ref[...], preferred_element_type=F32)
    pb = jnp.dot(ub, wb_ref[...], preferred_element_type=F32)
    merged = _sigmoid(ga) * pa + _sigmoid(gb) * pb
    h = x_ref[...] + jnp.dot(merged.astype(BF16), wo_ref[...], preferred_element_type=F32)
    ms = jnp.mean(h * h, axis=-1, keepdims=True)
    o_ref[...] = (h * lax.rsqrt(ms + EPS)) * fg_ref[...]


def _merge(x2, ya2, yb2, gates, wa, wb, wo, fg, tm):
    n, d = x2.shape
    n_gate = gates.shape[1]
    grid = (n // tm,)
    row = lambda i: (i, 0)
    const = lambda i: (0, 0)
    in_specs = [
        pl.BlockSpec((tm, d), row),
        pl.BlockSpec((tm, WIDTH), row),
        pl.BlockSpec((tm, WIDTH), row),
        pl.BlockSpec((tm, n_gate), row),
        pl.BlockSpec(wa.shape, const),
        pl.BlockSpec(wb.shape, const),
        pl.BlockSpec(wo.shape, const),
        pl.BlockSpec((1, d), const),
    ]
    blk = 2 * (2 * _nbytes((tm, d), F32) + 2 * _nbytes((tm, WIDTH), F32)
               + _nbytes((tm, n_gate), F32) + _nbytes(wa.shape, BF16) + _nbytes(wb.shape, BF16)
               + _nbytes(wo.shape, BF16))
    return pl.pallas_call(
        _merge_kernel, out_shape=jax.ShapeDtypeStruct((n, d), F32),
        grid=grid, in_specs=in_specs, out_specs=pl.BlockSpec((tm, d), row), name="merge",
        compiler_params=pltpu.CompilerParams(
            dimension_semantics=("arbitrary",), vmem_limit_bytes=_vmem_limit(blk)),
    )(x2, ya2, yb2, gates, wa, wb, wo, fg)


def _toeplitz(g, nrows, ncols):
    period = nrows + ncols
    lead = g.shape[:-1]
    p = jnp.concatenate([g[..., ::-1], jnp.zeros(lead + (1,), g.dtype)], axis=-1)
    flat = jnp.tile(p, (1,) * len(lead) + (nrows,))[..., :nrows * (period - 1)]
    x = flat.reshape(lead + (nrows, period - 1))
    return x[..., nrows - 1:nrows - 1 + ncols]


def _band_bias(a_rel_bias):
    pad = A_LEFT_CHUNKS * CHUNK
    rel = np.arange(BAND_TQ + BAND_WIN - 1) - (BAND_WIN - 1) + pad
    idx = np.clip(rel, -A_REL_CLIP, A_REL_CLIP) + A_REL_CLIP
    lo, hi = int(idx[0]), int(idx.max())
    n_flat = int((idx == hi).sum()) - 1
    assert np.array_equal(idx, np.minimum(np.arange(lo, lo + idx.size), hi))
    ab = a_rel_bias.astype(F32)
    g = jnp.concatenate([ab[:, lo:hi + 1], jnp.broadcast_to(ab[:, hi:hi + 1], (HEADS, n_flat))],
                        axis=1)
    bias = _toeplitz(g, BAND_TQ, BAND_WIN)
    qq = np.arange(BAND_TQ)[:, None]
    jj = np.arange(BAND_WIN)[None, :]
    in_band = np.logical_and(jj // CHUNK >= qq // CHUNK, jj // CHUNK <= qq // CHUNK + A_LEFT_CHUNKS)
    return jnp.where(jnp.asarray(in_band)[None], bias, NEG)


def _dsa_bias(t5_bias):
    T = DSA_T
    far = T5_BUCKETS // 2 - 1
    tb = t5_bias.astype(F32)
    slabs = [jnp.zeros((HEADS, T, T), F32)]
    for off in (-T, 0):
        rel = jnp.arange(2 * T - 1, dtype=jnp.int32) - (T - 1) + off
        onehot = (_t5_bucket(rel)[:, None] == jnp.arange(T5_BUCKETS)[None, :]).astype(F32)
        g = jnp.dot(onehot, (tb - tb[far]) * LOG2E, precision=lax.Precision.HIGHEST)
        slabs.append(_toeplitz(g.T, T, T))
    return jnp.stack(slabs)


def _far_bucket_is_constant(s):
    half = T5_BUCKETS // 2
    max_exact = half // 2
    for n in (DSA_T + 1, max(s - 1, DSA_T + 1)):
        large = max_exact + int(math.log(n / max_exact) / math.log(T5_MAX_DIST / max_exact)
                                * (half - max_exact))
        if min(large, half - 1) != half - 1:
            return False
    return True


def kernel(x, norm_gain, w_in, a_rel_bias, t5_bias, w_a_out, w_b_out, w_out, final_gain):
    b, s, d = x.shape
    depth = w_in.shape[0]
    n = b * s
    assert s % DSA_T == 0 and s % BAND_TQ == 0
    assert _far_bucket_is_constant(s)
    tm = 512 if n % 512 == 0 else DSA_T
    n_sel = min(TOPK_MAX, s // 4)

    splits = (WIDTH,) * 4 + (WIDTH, HEAD_DIM, HEAD_DIM, WIDTH) + (WIDTH, HEAD_DIM, HEADS) + (d, d)
    offs = np.concatenate([[0], np.cumsum(splits)])
    assert offs[-1] == w_in.shape[2]

    abias = _band_bias
    gbias = _dsa_bias(t5_bias)
    kk = jnp.arange(DSA_T)[:, None] // CHUNK
    qq = jnp.arange(DSA_T)[None, :] // CHUNK
    adm = jnp.stack([jnp.zeros((DSA_T, DSA_T), F32),
                     jnp.where(kk <= qq, 0.0, -jnp.inf).astype(F32)])

    h2 = x.reshape(n, d)
    for l in range(depth):
        w = w_in[l]
        (qa, ka, va, za, qb, kb, vb, zb, qi, ki, wi, ga, gb) = [
            w[:, offs[i]:offs[i + 1]] for i in range(len(splits))]
        wn = jnp.concatenate([qa, ka, va, ki, kb], axis=1).astype(BF16)
        wg = jnp.concatenate([za, zb, ga, gb], axis=1).astype(BF16)
        wt = jnp.concatenate([qb * (HEAD_DIM ** -0.5 * LOG2E), qi, vb, wi,
                              jnp.zeros((d, HEADS), w.dtype)], axis=1).T.astype(BF16)

        qkv, ki_o, kb_o, gates, qbT, qiT, vbT, wiT = _proj(
            h2, norm_gain[l].reshape(1, d).astype(F32), wn, wg, wt, tm)

        ya = _band(qkv.reshape(b, s, 3 * WIDTH), abias(a_rel_bias[l]))
        yb = _dsa(qbT, qiT, wiT, ki_o.reshape(b, s, HEAD_DIM), kb_o.reshape(b, s, HEAD_DIM),
                  vbT.reshape(b, s // DSA_T, HEAD_DIM, DSA_T), gbias, adm, n_sel)

        assert depth == 1
        h2 = _merge(h2, ya.reshape(n, WIDTH), yb.reshape(n, WIDTH), gates,
                    w_a_out[l].astype(BF16), w_b_out[l].astype(BF16), w_out[l].astype(BF16),
                    final_gain.reshape(1, d).astype(F32), tm)
    return h2.reshape(b, s, d)
```

```python
import math

import numpy as np
import jax
import jax.numpy as jnp
from jax import lax
from jax.experimental import pallas as pl
from jax.experimental.pallas import tpu as pltpu

F32 = jnp.float32
BF16 = jnp.bfloat16

CHUNK = 64
EPS = 1e-6
HEADS = 8
HEAD_DIM = 64
WIDTH = HEADS * HEAD_DIM
A_LEFT_CHUNKS = 8
A_REL_CLIP = 256
TOPK_MAX = 256
T5_BUCKETS = 32
T5_MAX_DIST = 128
NEG = -1e30
F32_MIN = float(np.finfo(np.float32).min)
LOG2E = math.log2(math.e)

V7X_VMEM_BYTES = 64 * 1024 * 1024
V7X_LANES = 128
VMEM_CAP_BYTES = 60000 * 1024
COMPILER_TEMP_BYTES = 8 * 1024 * 1024

BAND_TQ = 2 * CHUNK
BAND_WIN = (A_LEFT_CHUNKS + 2) * CHUNK
BAND_NBLK = BAND_WIN // BAND_TQ
DSA_T = 256
VX_ROWS = HEAD_DIM + 16
BISECT_MAX_ITERS = 26


def _vmem_limit(block_bytes):
    return int(min(block_bytes + COMPILER_TEMP_BYTES, VMEM_CAP_BYTES))


def _nbytes(shape, dtype):
    return int(np.prod(shape)) * jnp.dtype(dtype).itemsize


def _t5_bucket(rel):
    half = T5_BUCKETS // 2
    max_exact = half // 2
    ret = jnp.where(rel > 0, half, 0)
    n = jnp.abs(rel)
    nf = jnp.maximum(n, 1).astype(F32)
    large = max_exact + (jnp.log(nf / max_exact) / math.log(T5_MAX_DIST / max_exact)
                         * (half - max_exact)).astype(jnp.int32)
    large = jnp.minimum(large, half - 1)
    return ret + jnp.where(n < max_exact, n, large)


def _sigmoid(v):
    return 1.0 / (1.0 + jnp.exp(-v))


def _proj_kernel(x_ref, g_ref, wn_ref, wg_ref, wt_ref,
                 ka_ref, ki_ref, kb_ref, gate_ref,
                 qaT_ref, vaT_ref, qbT_ref, qiT_ref, vbT_ref, wiT_ref):
    xf = x_ref[...]
    ms = jnp.mean(xf * xf, axis=-1, keepdims=True)
    hn = ((xf * lax.rsqrt(ms + EPS)) * g_ref[...]).astype(BF16)
    tm = hn.shape[0]

    ka_ref[...] = jnp.dot(hn, wn_ref[:, 0:WIDTH], preferred_element_type=F32).astype(BF16)
    kk = jnp.dot(hn, wn_ref[:, WIDTH:WIDTH + 2 * HEAD_DIM], preferred_element_type=F32)
    ki_ref[...] = kk[:, :HEAD_DIM].astype(BF16)
    kb_ref[...] = kk[:, HEAD_DIM:].astype(BF16)
    n_gate = gate_ref.shape[1]
    for c0 in range(0, n_gate, WIDTH):
        gate_ref[:, c0:c0 + WIDTH] = jnp.dot(
            hn, wg_ref[:, c0:c0 + WIDTH], preferred_element_type=F32)

    def t_rows(r0, nrows):
        return lax.dot_general(wt_ref[r0:r0 + nrows, :], hn, (((1,), (1,)), ((), ())),
                               preferred_element_type=F32)
    qaT_ref[...] = t_rows(0, WIDTH).astype(BF16)
    vaT = t_rows(WIDTH, WIDTH).astype(BF16)
    for s in range(tm // BAND_TQ):
        vaT_ref[s] = vaT[:, s * BAND_TQ:(s + 1) * BAND_TQ]
    qbT_ref[...] = t_rows(2 * WIDTH, WIDTH).astype(BF16)
    qiT_ref[...] = t_rows(3 * WIDTH, WIDTH).astype(BF16)
    tail = t_rows(4 * WIDTH, HEAD_DIM + 2 * HEADS)
    for s in range(tm // DSA_T):
        vbT_ref[s] = tail[0:HEAD_DIM, s * DSA_T:(s + 1) * DSA_T].astype(BF16)
    wiT_ref[...] = tail[HEAD_DIM:HEAD_DIM + HEADS, :]


def _proj(x2, gain, wn, wg, wt, tm):
    n, d = x2.shape
    n_gate = wg.shape[1]
    grid = (n // tm,)
    const = lambda i: (0, 0)
    row = lambda i: (i, 0)
    col = lambda i: (0, i)
    in_specs = [
        pl.BlockSpec((tm, d), row),
        pl.BlockSpec((1, d), const),
        pl.BlockSpec(wn.shape, const),
        pl.BlockSpec(wg.shape, const),
        pl.BlockSpec(wt.shape, const),
    ]
    out_shape = (
        jax.ShapeDtypeStruct((n, WIDTH), BF16),
        jax.ShapeDtypeStruct((n, HEAD_DIM), BF16),
        jax.ShapeDtypeStruct((n, HEAD_DIM), BF16),
        jax.ShapeDtypeStruct((n, n_gate), F32),
        jax.ShapeDtypeStruct((WIDTH, n), BF16),
        jax.ShapeDtypeStruct((n // BAND_TQ, WIDTH, BAND_TQ), BF16),
        jax.ShapeDtypeStruct((WIDTH, n), BF16),
        jax.ShapeDtypeStruct((WIDTH, n), BF16),
        jax.ShapeDtypeStruct((n // DSA_T, HEAD_DIM, DSA_T), BF16),
        jax.ShapeDtypeStruct((HEADS, n), F32),
    )
    out_specs = (
        pl.BlockSpec((tm, WIDTH), row),
        pl.BlockSpec((tm, HEAD_DIM), row),
        pl.BlockSpec((tm, HEAD_DIM), row),
        pl.BlockSpec((tm, n_gate), row),
        pl.BlockSpec((WIDTH, tm), col),
        pl.BlockSpec((tm // BAND_TQ, WIDTH, BAND_TQ), lambda i: (i, 0, 0)),
        pl.BlockSpec((WIDTH, tm), col),
        pl.BlockSpec((WIDTH, tm), col),
        pl.BlockSpec((tm // DSA_T, HEAD_DIM, DSA_T), lambda i: (i, 0, 0)),
        pl.BlockSpec((HEADS, tm), col),
    )
    blk = 2 * (_nbytes((tm, d), F32) + _nbytes(wn.shape, BF16) + _nbytes(wg.shape, BF16)
               + _nbytes(wt.shape, BF16) + _nbytes((tm, WIDTH), BF16)
               + 2 * _nbytes((tm, V7X_LANES), BF16) + _nbytes((tm, n_gate), F32)
               + 4 * _nbytes((WIDTH, tm), BF16) + _nbytes((HEAD_DIM, tm), BF16)
               + _nbytes((HEADS, tm), F32))
    return pl.pallas_call(
        _proj_kernel, out_shape=out_shape, grid=grid, in_specs=in_specs, out_specs=out_specs,
        name="proj",
        compiler_params=pltpu.CompilerParams(
            dimension_semantics=("arbitrary",), vmem_limit_bytes=_vmem_limit(blk)),
    )(x2, gain, wn, wg, wt)


def _band_kernel(qT_ref, k_ref, vT_ref, bias_ref, o_ref, s_ref, p_ref):
    t = pl.program_id(1)
    row = lax.broadcasted_iota(jnp.int32, (V7X_LANES, BAND_TQ), 0)
    first_half = row < HEAD_DIM
    blocks = []
    for i in range(BAND_NBLK):
        kidx = t - (BAND_NBLK - 1) + i
        kc = jnp.maximum(kidx, 0)
        blocks.append((kidx, kc, pl.multiple_of(kc * BAND_TQ, BAND_TQ)))

    m = []
    for h in range(HEADS):
        cs = slice(V7X_LANES * (h // 2), V7X_LANES * (h // 2 + 1))
        qT = qT_ref[cs, :]
        qe = jnp.where(first_half if h % 2 == 0 else jnp.logical_not(first_half), qT,
                       jnp.zeros_like(qT))
        m_el = None
        for i, (kidx, kc, ks) in enumerate(blocks):
            ksl = slice(BAND_TQ * i, BAND_TQ * (i + 1))
            s = jnp.dot(k_ref[pl.ds(ks, BAND_TQ), cs], qe, preferred_element_type=F32)
            s = s + bias_ref[h, ksl, :]
            if i < BAND_NBLK - 1:
                s = jnp.where(kidx >= 0, s, NEG)
            s_ref[h, ksl, :] = s
            m_el = s if m_el is None else jnp.maximum(m_el, s)
        m.append(jnp.max(m_el, axis=0, keepdims=True))

    l = []
    for h in range(HEADS):
        pr = jnp.exp2(s_ref[h] - m[h])
        l.append(jnp.sum(pr, axis=0, keepdims=True))
        p_ref[h] = pr.astype(BF16)

    outs = []
    for h in range(HEADS):
        acc = None
        for i, (kidx, kc, ks) in enumerate(blocks):
            vT = vT_ref[kc, HEAD_DIM * h:HEAD_DIM * (h + 1), :]
            pv = jnp.dot(vT, p_ref[h, BAND_TQ * i:BAND_TQ * (i + 1), :],
                         preferred_element_type=F32)
            acc = pv if acc is None else acc + pv
        outs.append(acc * (1.0 / l[h]))
    o_ref[...] = jnp.concatenate(outs, axis=0).T


def _band(qaT, ka3, vaT4, abiasT):
    b, s, _ = ka3.shape
    nt = s // BAND_TQ
    grid = (b, nt)
    in_specs = [
        pl.BlockSpec((WIDTH, BAND_TQ), lambda bi, t: (0, bi * nt + t)),
        pl.BlockSpec((None, s, WIDTH), lambda bi, t: (bi, 0, 0)),
        pl.BlockSpec((None, nt, WIDTH, BAND_TQ), lambda bi, t: (bi, 0, 0, 0)),
        pl.BlockSpec(abiasT.shape, lambda bi, t: (0, 0, 0)),
    ]
    out_specs = pl.BlockSpec((None, BAND_TQ, WIDTH), lambda bi, t: (bi, t, 0))
    scratch = [
        pltpu.VMEM((HEADS, BAND_WIN, BAND_TQ), F32),
        pltpu.VMEM((HEADS, BAND_WIN, BAND_TQ), BF16),
    ]
    blk = (2 * (_nbytes((WIDTH, BAND_TQ), BF16) + 2 * _nbytes((s, WIDTH), BF16)
                + _nbytes(abiasT.shape, F32) + _nbytes((BAND_TQ, WIDTH), F32))
           + _nbytes((HEADS, BAND_WIN, BAND_TQ), F32) + _nbytes((HEADS, BAND_WIN, BAND_TQ), BF16))
    return pl.pallas_call(
        _band_kernel, out_shape=jax.ShapeDtypeStruct((b, s, WIDTH), F32),
        grid=grid, in_specs=in_specs, out_specs=out_specs, scratch_shapes=scratch, name="band",
        compiler_params=pltpu.CompilerParams(
            dimension_semantics=("arbitrary", "arbitrary"), vmem_limit_bytes=_vmem_limit(blk)),
    )(qaT, ka3, vaT4, abiasT)


def _make_dsa_kernel(n_sel):
    T = DSA_T
    kf = float(n_sel)
    idx_scale = (HEADS ** -0.5) * (HEAD_DIM ** -0.5)

    def kernel(qbT_ref, qiT_ref, wiT_ref, ki_ref, kb_ref, vbT_ref, gbias_ref, adm_ref, o_ref,
               sc_ref, m_ref, acc_ref, tmax_ref, s_ref, p_ref, vx_ref):
        j = pl.program_id(1)
        nt = j + 1
        zrow = jnp.zeros((1, T), F32)

        def tile_rows(kt):
            return pl.ds(pl.multiple_of(kt * T, T), T)

        wis = wiT_ref[...] * idx_scale

        def fold8(a):
            return jnp.sum(a.reshape(a.shape[0] // 8, 8, T), axis=0)

        HT = T // 2

        raw_ref = s_ref.at[0]

        def p1_half(kt_dot, kt_post, half, carry):
            hs = slice(half * HT, (half + 1) * HT)
            if kt_post is not None:
                mn, mx, cgt0, cge0 = carry
                acc = raw_ref[hs, :]
                mn = jnp.minimum(mn, jnp.min(acc, axis=0, keepdims=True))
                mx = jnp.maximum(mx, jnp.max(acc, axis=0, keepdims=True))
                is_diag = (kt_post == j).astype(jnp.int32)
                sc = acc + adm_ref[is_diag, hs, :]
                sc_ref[pl.ds(pl.multiple_of(kt_post * T + half * HT, HT), HT), :] = sc
                cgt0 = cgt0 + fold8(jnp.where(sc > 0.0, 1.0, 0.0))
                cge0 = cge0 + fold8(jnp.where(sc >= 0.0, 1.0, 0.0))
                carry = (mn, mx, cgt0, cge0)
            if kt_dot is not None:
                ki_t = ki_ref[pl.ds(pl.multiple_of(kt_dot * T + half * HT, HT), HT), :]
                acc = jnp.zeros((HT, T), F32)
                for h in range(HEADS):
                    lg = jnp.dot(ki_t, qiT_ref[HEAD_DIM * h:HEAD_DIM * (h + 1), :],
                                 preferred_element_type=F32)
                    acc = acc + jnp.maximum(lg, 0.0) * wis[h:h + 1, :]
                raw_ref[hs, :] = acc
            return carry

        def p1(kt, carry):
            for half in range(2):
                carry = p1_half(kt, kt - 1, half, carry)
            return carry

        z8 = jnp.zeros((8, T), F32)
        carry = (jnp.full((1, T), jnp.inf, F32), jnp.full((1, T), -jnp.inf, F32), z8, z8)
        for half in range(2):
            p1_half(0, None, half, None)
        carry = lax.fori_loop(1, nt, p1, carry)
        for half in range(2):
            carry = p1_half(None, nt - 1, half, carry)
        lo0, hi0, cgt0, cge0 = carry
        cgt0 = jnp.sum(cgt0, axis=0, keepdims=True)
        cge0 = jnp.sum(cge0, axis=0, keepdims=True)

        qpos = j * T + lax.broadcasted_iota(jnp.int32, (1, T), 1)
        n_adm = ((qpos // CHUNK + 1) * CHUNK).astype(F32)
        small = n_adm <= kf
        above0 = cgt0 >= kf
        below0 = cge0 < kf
        zero_tie = jnp.logical_not(jnp.logical_or(above0, below0))
        lo_pos = lo0 > 0.0
        lo1 = jnp.where(zero_tie, 0.0, jnp.where(jnp.logical_and(above0, jnp.logical_not(lo_pos)),
                                                 0.0, lo0))
        cl1 = jnp.where(jnp.logical_or(zero_tie, jnp.logical_and(above0, jnp.logical_not(lo_pos))),
                        cge0, n_adm)
        hi1 = jnp.where(zero_tie, 0.0, jnp.where(below0, jnp.minimum(hi0, 0.0), hi0))

        @pl.when(nt % 2 == 1)
        def _():
            sc_ref[tile_rows(nt), :] = jnp.full((T, T), -jnp.inf, F32)

        def count_ge(th):
            def body(i, c8):
                for u in range(2):
                    blk = sc_ref[tile_rows(2 * i + u), :]
                    c8 = c8 + fold8(jnp.where(blk >= th, 1.0, 0.0))
                return c8
            return jnp.sum(lax.fori_loop(0, (nt + 1) // 2, body, z8), axis=0, keepdims=True)

        def any_open(done):
            return jnp.max(jnp.where(done, 0.0, 1.0)) > 0.0

        def settled(cl_):
            return jnp.logical_or(jnp.logical_or(small, zero_tie), cl_ == kf)

        def b_cond(st):
            it, _, _, _, open_ = st
            return jnp.logical_and(it < BISECT_MAX_ITERS, open_)

        def b_body(st):
            it, lo, hi, cl, _ = st
            mid = 0.5 * lo + 0.5 * hi
            c = count_ge(mid)
            ge = c >= kf
            lo = jnp.where(ge, mid, lo)
            cl = jnp.where(ge, c, cl)
            hi = jnp.where(ge, hi, mid)
            return it + 1, lo, hi, cl, any_open(settled(cl))

        _, lo, hi, cl, open_ = lax.while_loop(
            b_cond, b_body, (jnp.int32(0), lo1, hi1, cl1, any_open(settled(cl1))))
        has_ties = jnp.logical_or(open_, jnp.max(jnp.where(
            jnp.logical_and(zero_tie, jnp.logical_not(small)), 1.0, 0.0)) > 0.0)

        def fast_mask():
            thr = jnp.where(small, F32_MIN, lo)

            def body(kt, _):
                blk = sc_ref[tile_rows(kt), :]
                sc_ref[tile_rows(kt), :] = jnp.where(blk >= thr, 0.0, NEG)
                return 0
            lax.fori_loop(0, nt, body, 0)

        def exact_mask():
            def done_of(lo_, hi_, cl_):
                return jnp.logical_or(jnp.logical_or(small, cl_ == kf), lo_ >= hi_)

            def s_body(st):
                lo_, hi_, cl_, _ = st
                done = done_of(lo_, hi_, cl_)
                mid = 0.5 * lo_ + 0.5 * hi_
                mid = jnp.where(mid > lo_, mid, hi_)

                def body(kt, c3):
                    c, mn_ge, mx_lt = c3
                    blk = sc_ref[tile_rows(kt), :]
                    ge_ = blk >= mid
                    c = c + jnp.sum(jnp.where(ge_, 1.0, 0.0), axis=0, keepdims=True)
                    mn_ge = jnp.minimum(mn_ge, jnp.min(jnp.where(ge_, blk, jnp.inf),
                                                       axis=0, keepdims=True))
                    mx_lt = jnp.maximum(mx_lt, jnp.max(jnp.where(ge_, -jnp.inf, blk),
                                                       axis=0, keepdims=True))
                    return c, mn_ge, mx_lt
                c, mn_ge, mx_lt = lax.fori_loop(
                    0, nt, body, (zrow, jnp.full((1, T), jnp.inf, F32),
                                  jnp.full((1, T), -jnp.inf, F32)))
                ge = c >= kf
                lo_n = jnp.where(done, lo_, jnp.where(ge, mn_ge, lo_))
                cl_n = jnp.where(done, cl_, jnp.where(ge, c, cl_))
                hi_n = jnp.where(done, hi_, jnp.where(ge, hi_, mx_lt))
                return lo_n, hi_n, cl_n, any_open(done_of(lo_n, hi_n, cl_n))

            lo_e, _, cl_e, _ = lax.while_loop(
                lambda st: st[3], s_body, (lo, hi, cl, any_open(done_of(lo, hi, cl))))
            thr = jnp.where(small, F32_MIN, lo_e)

            def count_gt():
                def body(kt, c8):
                    blk = sc_ref[tile_rows(kt), :]
                    return c8 + fold8(jnp.where(blk > thr, 1.0, 0.0))
                return jnp.sum(lax.fori_loop(0, nt, body, z8), axis=0, keepdims=True)
            cgt = lax.cond(open_, count_gt, lambda: cgt0)
            need = jnp.where(jnp.logical_or(small, cl_e == kf), jnp.inf, kf - cgt)

            UNDECIDED = -1.0

            def m_body(kt, st):
                carry, xt, xc = st
                blk = sc_ref[tile_rows(kt), :]
                eq = jnp.where(blk == thr, 1.0, 0.0)
                cnt = jnp.sum(fold8(eq), axis=0, keepdims=True)
                has_budget = carry < need
                crossing = jnp.logical_and(has_budget, carry + cnt > need)
                tie_val = jnp.where(crossing, UNDECIDED, jnp.where(has_budget, 0.0, NEG))
                sc_ref[tile_rows(kt), :] = jnp.where(
                    blk > thr, 0.0, jnp.where(blk == thr, tie_val, NEG))
                xt = jnp.where(crossing, kt.astype(F32), xt)
                xc = jnp.where(crossing, carry, xc)
                return carry + cnt, xt, xc
            _, xt, xc = lax.fori_loop(0, nt, m_body, (zrow, jnp.full((1, T), -1.0, F32), zrow))

            def r_body(st):
                xt_, ktf = st
                kt = ktf.astype(jnp.int32)
                msk = sc_ref[tile_rows(kt), :]
                und = msk == UNDECIDED
                r_io = lax.broadcasted_iota(jnp.int32, (T, T), 0)
                c_io = lax.broadcasted_iota(jnp.int32, (T, T), 1)
                ltri = jnp.where(c_io < r_io, 1.0, 0.0).astype(BF16)
                rank = jnp.dot(ltri, jnp.where(und, 1.0, 0.0).astype(BF16),
                               preferred_element_type=F32) + xc
                sc_ref[tile_rows(kt), :] = jnp.where(
                    und, jnp.where(rank < need, 0.0, NEG), msk)
                xt_ = jnp.where(xt_ == ktf, -1.0, xt_)
                return xt_, jnp.max(xt_)
            lax.while_loop(lambda st: st[1] >= 0.0, r_body, (xt, jnp.max(xt)))

        lax.cond(has_ties, exact_mask, fast_mask)

        m_ref[...] = jnp.full(m_ref.shape, NEG, F32)
        acc_ref[...] = jnp.zeros(acc_ref.shape, F32)
        vx_ref[HEAD_DIM:, :] = jnp.ones((VX_ROWS - HEAD_DIM, T), BF16)

        def step(kt_a, bias_slab, kt_bc):
            if kt_bc is not None:
                vx_ref[0:HEAD_DIM, :] = vbT_ref[kt_bc]
                m_old = m_ref[...]
                m_new = jnp.maximum(m_old, tmax_ref[...])
                alpha = jnp.exp2(m_old - m_new)
                m_ref[...] = m_new
            if kt_a is not None:
                kb_t = kb_ref[tile_rows(kt_a), :]
                msk = sc_ref[tile_rows(kt_a), :]
            tmax = []
            for h in range(HEADS):
                rows = slice(HEAD_DIM * h, HEAD_DIM * (h + 1))
                if kt_bc is not None:
                    p_ref[h] = jnp.exp2(s_ref[h] - m_new[h:h + 1, :]).astype(BF16)
                if kt_a is not None:
                    add = msk if bias_slab is None else msk + gbias_ref[bias_slab, h]
                    s = jnp.dot(kb_t, qbT_ref[rows, :], preferred_element_type=F32) + add
                    s_ref[h] = s
                    tmax.append(jnp.max(s, axis=0, keepdims=True))
                if kt_bc is not None:
                    acc_ref[h] = alpha[h:h + 1, :] * acc_ref[h] + jnp.dot(
                        vx_ref[...], p_ref[h], preferred_element_type=F32)
            if kt_a is not None:
                tmax_ref[...] = jnp.concatenate(tmax, axis=0)

        n_far = jnp.maximum(j - 1, 0)
        step(0, jnp.clip(2 - j, 0, 2), None)

        def far_step(a, _):
            step(a, None, a - 1)
            return 0

        def near_step(a, _):
            step(a, a - j + 2, a - 1)
            return 0

        lax.fori_loop(1, n_far, far_step, 0)
        lax.fori_loop(jnp.maximum(n_far, 1), nt, near_step, 0)
        step(None, None, nt - 1)

        outs = []
        for h in range(HEADS):
            a = acc_ref[h]
            outs.append(a[0:HEAD_DIM, :] * (1.0 / a[HEAD_DIM:HEAD_DIM + 1, :]))
        o_ref[...] = jnp.concatenate(outs, axis=0).T

    return kernel


def _dsa(qbT, qiT, wiT, ki3, kb3, vbT4, gbias, adm, n_sel):
    b, s, _ = ki3.shape
    T = DSA_T
    nq = s // T
    grid = (b, nq)
    in_specs = [
        pl.BlockSpec((WIDTH, T), lambda bi, j: (0, bi * nq + j)),
        pl.BlockSpec((WIDTH, T), lambda bi, j: (0, bi * nq + j)),
        pl.BlockSpec((HEADS, T), lambda bi, j: (0, bi * nq + j)),
        pl.BlockSpec((None, s, HEAD_DIM), lambda bi, j: (bi, 0, 0)),
        pl.BlockSpec((None, s, HEAD_DIM), lambda bi, j: (bi, 0, 0)),
        pl.BlockSpec((None, nq, HEAD_DIM, T), lambda bi, j: (bi, 0, 0, 0)),
        pl.BlockSpec(gbias.shape, lambda bi, j: (0, 0, 0, 0)),
        pl.BlockSpec(adm.shape, lambda bi, j: (0, 0, 0)),
    ]
    out_specs = pl.BlockSpec((None, T, WIDTH), lambda bi, j: (bi, j, 0))
    scratch = [
        pltpu.VMEM((s + T, T), F32),
        pltpu.VMEM((HEADS, T), F32),
        pltpu.VMEM((HEADS, VX_ROWS, T), F32),
        pltpu.VMEM((HEADS, T), F32),
        pltpu.VMEM((HEADS, T, T), F32),
        pltpu.VMEM((HEADS, T, T), BF16),
        pltpu.VMEM((VX_ROWS, T), BF16),
    ]
    blk = (2 * (2 * _nbytes((WIDTH, T), BF16) + _nbytes((HEADS, T), F32)
                + 2 * _nbytes((s, V7X_LANES), BF16) + _nbytes((HEAD_DIM, s), BF16)
                + _nbytes(gbias.shape, F32) + _nbytes(adm.shape, F32)
                + _nbytes((T, WIDTH), F32))
           + _nbytes((s + T, T), F32) + 2 * _nbytes((HEADS, T), F32)
           + _nbytes((HEADS, VX_ROWS, T), F32) + _nbytes((HEADS, T, T), F32)
           + _nbytes((HEADS, T, T), BF16) + _nbytes((VX_ROWS, T), BF16))
    return pl.pallas_call(
        _make_dsa_kernel(n_sel), out_shape=jax.ShapeDtypeStruct((b, s, WIDTH), F32),
        grid=grid, in_specs=in_specs, out_specs=out_specs, scratch_shapes=scratch, name="dsa",
        compiler_params=pltpu.CompilerParams(
            dimension_semantics=("arbitrary", "arbitrary"), vmem_limit_bytes=_vmem_limit(blk)),
    )(qbT, qiT, wiT, ki3, kb3, vbT4, gbias, adm)


def _merge_kernel(x_ref, ya_ref, yb_ref, gate_ref, wa_ref, wb_ref, wo_ref, fg_ref, o_ref):
    d = x_ref.shape[1]
    za = gate_ref[:, 0:WIDTH]
    zb = gate_ref[:, WIDTH:2 * WIDTH]
    ga = gate_ref[:, 2 * WIDTH:2 * WIDTH + d]
    gb = gate_ref[:, 2 * WIDTH + d:2 * WIDTH + 2 * d]
    ua = (ya_ref[...] * (za * _sigmoid(za))).astype(BF16)
    ub = (yb_ref[...] * (zb * _sigmoid(zb))).astype(BF16)
    pa = jnp.dot(ua, wa_ref[...], preferred_element_type=F32)
    pb = jnp.dot(ub, wb_ref[...], preferred_element_type=F32)
    merged = _sigmoid(ga) * pa + _sigmoid(gb) * pb
    h = x_ref[...] + jnp.dot(merged.astype(BF16), wo_ref[...], preferred_element_type=F32)
    ms = jnp.mean(h * h, axis=-1, keepdims=True)
    o_ref[...] = (h * lax.rsqrt(ms + EPS)) * fg_ref[...]


def _merge(x2, ya2, yb2, gates, wa, wb, wo, fg, tm):
    n, d = x2.shape
    n_gate = gates.shape[1]
    grid = (n // tm,)
    row = lambda i: (i, 0)
    const = lambda i: (0, 0)
    in_specs = [
        pl.BlockSpec((tm, d), row),
        pl.BlockSpec((tm, WIDTH), row),
        pl.BlockSpec((tm, WIDTH), row),
        pl.BlockSpec((tm, n_gate), row),
        pl.BlockSpec(wa.shape, const),
        pl.BlockSpec(wb.shape, const),
        pl.BlockSpec(wo.shape, const),
        pl.BlockSpec((1, d), const),
    ]
    blk = 2 * (2 * _nbytes((tm, d), F32) + 2 * _nbytes((tm, WIDTH), F32)
               + _nbytes((tm, n_gate), F32) + _nbytes(wa.shape, BF16) + _nbytes(wb.shape, BF16)
               + _nbytes(wo.shape, BF16))
    return pl.pallas_call(
        _merge_kernel, out_shape=jax.ShapeDtypeStruct((n, d), F32),
        grid=grid, in_specs=in_specs, out_specs=pl.BlockSpec((tm, d), row), name="merge",
        compiler_params=pltpu.CompilerParams(
            dimension_semantics=("arbitrary",), vmem_limit_bytes=_vmem_limit(blk)),
    )(x2, ya2, yb2, gates, wa, wb, wo, fg)


def _toeplitz(g, nrows, ncols):
    period = nrows + ncols
    lead = g.shape[:-1]
    p = jnp.concatenate([g[..., ::-1], jnp.zeros(lead + (1,), g.dtype)], axis=-1)
    flat = jnp.tile(p, (1,) * len(lead) + (nrows,))[..., :nrows * (period - 1)]
    x = flat.reshape(lead + (nrows, period - 1))
    return x[..., nrows - 1:nrows - 1 + ncols]


def _band_bias(a_rel_bias):
    pad = A_LEFT_CHUNKS * CHUNK
    rel = np.arange(BAND_TQ + BAND_WIN - 1) - (BAND_WIN - 1) + pad
    idx = np.clip(rel, -A_REL_CLIP, A_REL_CLIP) + A_REL_CLIP
    lo, hi = int(idx[0]), int(idx.max())
    n_flat = int((idx == hi).sum()) - 1
    assert np.array_equal(idx, np.minimum(np.arange(lo, lo + idx.size), hi))
    ab = a_rel_bias.astype(F32)
    g = jnp.concatenate([ab[:, lo:hi + 1], jnp.broadcast_to(ab[:, hi:hi + 1], (HEADS, n_flat))],
                        axis=1)
    bias = _toeplitz(g, BAND_TQ, BAND_WIN)
    qq = np.arange(BAND_TQ)[:, None]
    jj = np.arange(BAND_WIN)[None, :]
    in_band = np.logical_and(jj // CHUNK >= qq // CHUNK, jj // CHUNK <= qq // CHUNK + A_LEFT_CHUNKS)
    bias = jnp.where(jnp.asarray(in_band)[None], bias * LOG2E, NEG)
    return jnp.swapaxes(bias, 1, 2)


def _dsa_bias(t5_bias):
    T = DSA_T
    far = T5_BUCKETS // 2 - 1
    tb = t5_bias.astype(F32)
    slabs = [jnp.zeros((HEADS, T, T), F32)]
    for off in (-T, 0):
        rel = jnp.arange(2 * T - 1, dtype=jnp.int32) - (T - 1) + off
        g = (tb[_t5_bucket(rel)] - tb[far]) * LOG2E
        slabs.append(_toeplitz(g.T, T, T))
    return jnp.stack(slabs)


def _far_bucket_is_constant(s):
    half = T5_BUCKETS // 2
    max_exact = half // 2
    for n in (DSA_T + 1, max(s - 1, DSA_T + 1)):
        large = max_exact + int(math.log(n / max_exact) / math.log(T5_MAX_DIST / max_exact)
                                * (half - max_exact))
        if min(large, half - 1) != half - 1:
            return False
    return True


def kernel(x, norm_gain, w_in, a_rel_bias, t5_bias, w_a_out, w_b_out, w_out, final_gain):
    b, s, d = x.shape
    depth = w_in.shape[0]
    n = b * s
    assert s % DSA_T == 0 and s % BAND_TQ == 0
    assert _far_bucket_is_constant(s)
    tm = 512 if n % 512 == 0 else DSA_T
    n_sel = min(TOPK_MAX, s // 4)

    splits = (WIDTH,) * 4 + (WIDTH, HEAD_DIM, HEAD_DIM, WIDTH) + (WIDTH, HEAD_DIM, HEADS) + (d, d)
    offs = np.concatenate([[0], np.cumsum(splits)])
    assert offs[-1] == w_in.shape[2]

    abias = _band_bias
    gbias = _dsa_bias(t5_bias)
    kk = jnp.arange(DSA_T)[:, None] // CHUNK
    qq = jnp.arange(DSA_T)[None, :] // CHUNK
    adm = jnp.stack([jnp.zeros((DSA_T, DSA_T), F32),
                     jnp.where(kk <= qq, 0.0, -jnp.inf).astype(F32)])

    h2 = x.reshape(n, d)
    for l in range(depth):
        w = w_in[l]
        (qa, ka, va, za, qb, kb, vb, zb, qi, ki, wi, ga, gb) = [
            w[:, offs[i]:offs[i + 1]] for i in range(len(splits))]
        wn = jnp.concatenate([ka, ki, kb], axis=1).astype(BF16)
        wg = jnp.concatenate([za, zb, ga, gb], axis=1).astype(BF16)
        qscale = HEAD_DIM ** -0.5 * LOG2E
        wt = jnp.concatenate([qa * qscale, va, qb * qscale, qi, vb, wi,
                              jnp.zeros((d, HEADS), w.dtype)], axis=1).T.astype(BF16)

        ka_o, ki_o, kb_o, gates, qaT, vaT, qbT, qiT, vbT, wiT = _proj(
            h2, norm_gain[l].reshape(1, d).astype(F32), wn, wg, wt, tm)

        ya = _band(qaT, ka_o.reshape(b, s, WIDTH),
                   vaT.reshape(b, s // BAND_TQ, WIDTH, BAND_TQ), abias(a_rel_bias[l]))
        yb = _dsa(qbT, qiT, wiT, ki_o.reshape(b, s, HEAD_DIM), kb_o.reshape(b, s, HEAD_DIM),
                  vbT.reshape(b, s // DSA_T, HEAD_DIM, DSA_T), gbias, adm, n_sel)

        assert depth == 1
        h2 = _merge(h2, ya.reshape(n, WIDTH), yb.reshape(n, WIDTH), gates,
                    w_a_out[l].astype(BF16), w_b_out[l].astype(BF16), w_out[l].astype(BF16),
                    final_gain.reshape(1, d).astype(F32), tm)
    return h2.reshape(b, s, d)
```

```python
import math

import numpy as np
import jax
import jax.numpy as jnp
from jax import lax
from jax.experimental import pallas as pl
from jax.experimental.pallas import tpu as pltpu

F32 = jnp.float32
BF16 = jnp.bfloat16

CHUNK = 64
EPS = 1e-6
HEADS = 8
HEAD_DIM = 64
WIDTH = HEADS * HEAD_DIM
A_LEFT_CHUNKS = 8
A_REL_CLIP = 256
TOPK_MAX = 256
T5_BUCKETS = 32
T5_MAX_DIST = 128
NEG = -1e30
F32_MIN = float(np.finfo(np.float32).min)
LOG2E = math.log2(math.e)

V7X_VMEM_BYTES = 64 * 1024 * 1024
V7X_LANES = 128
VMEM_CAP_BYTES = 60000 * 1024
COMPILER_TEMP_BYTES = 8 * 1024 * 1024

BAND_TQ = 2 * CHUNK
BAND_WIN = (A_LEFT_CHUNKS + 2) * CHUNK
BAND_NBLK = BAND_WIN // BAND_TQ
DSA_T = 256
VX_ROWS = HEAD_DIM + 16
PV_LAG = 2
BISECT_MAX_ITERS = 26


def _vmem_limit(block_bytes):
    return int(min(block_bytes + COMPILER_TEMP_BYTES, VMEM_CAP_BYTES))


def _nbytes(shape, dtype):
    return int(np.prod(shape)) * jnp.dtype(dtype).itemsize


def _t5_bucket(rel):
    half = T5_BUCKETS // 2
    max_exact = half // 2
    ret = jnp.where(rel > 0, half, 0)
    n = jnp.abs(rel)
    nf = jnp.maximum(n, 1).astype(F32)
    large = max_exact + (jnp.log(nf / max_exact) / math.log(T5_MAX_DIST / max_exact)
                         * (half - max_exact)).astype(jnp.int32)
    large = jnp.minimum(large, half - 1)
    return ret + jnp.where(n < max_exact, n, large)


def _sigmoid(v):
    return 1.0 / (1.0 + jnp.exp(-v))


def _proj_kernel(x_ref, g_ref, wn_ref, wg_ref, wt_ref,
                 ka_ref, ki_ref, kb_ref, gate_ref,
                 qaT_ref, vaT_ref, qbT_ref, qiT_ref, vbT_ref, wiT_ref):
    xf = x_ref[...]
    ms = jnp.mean(xf * xf, axis=-1, keepdims=True)
    hn = ((xf * lax.rsqrt(ms + EPS)) * g_ref[...]).astype(BF16)
    tm = hn.shape[0]

    ka_ref[...] = jnp.dot(hn, wn_ref[:, 0:WIDTH], preferred_element_type=F32).astype(BF16)
    kk = jnp.dot(hn, wn_ref[:, WIDTH:WIDTH + 2 * HEAD_DIM], preferred_element_type=F32)
    ki_ref[...] = kk[:, :HEAD_DIM].astype(BF16)
    kb_ref[...] = kk[:, HEAD_DIM:].astype(BF16)
    n_gate = gate_ref.shape[1]
    for c0 in range(0, n_gate, WIDTH):
        gate_ref[:, c0:c0 + WIDTH] = jnp.dot(
            hn, wg_ref[:, c0:c0 + WIDTH], preferred_element_type=F32).astype(BF16)

    def t_rows(r0, nrows):
        return lax.dot_general(wt_ref[r0:r0 + nrows, :], hn, (((1,), (1,)), ((), ())),
                               preferred_element_type=F32)
    qaT_ref[...] = t_rows(0, WIDTH).astype(BF16)
    vaT = t_rows(WIDTH, WIDTH).astype(BF16)
    for s in range(tm // BAND_TQ):
        vaT_ref[s] = vaT[:, s * BAND_TQ:(s + 1) * BAND_TQ]
    qbT_ref[...] = t_rows(2 * WIDTH, WIDTH).astype(BF16)
    qiT_ref[...] = t_rows(3 * WIDTH, WIDTH).astype(BF16)
    tail = t_rows(4 * WIDTH, HEAD_DIM + 2 * HEADS)
    for s in range(tm // DSA_T):
        vbT_ref[s] = tail[0:HEAD_DIM, s * DSA_T:(s + 1) * DSA_T].astype(BF16)
    wiT_ref[...] = tail[HEAD_DIM:HEAD_DIM + HEADS, :]


def _proj(x2, gain, wn, wg, wt, tm):
    n, d = x2.shape
    n_gate = wg.shape[1]
    grid = (n // tm,)
    const = lambda i: (0, 0)
    row = lambda i: (i, 0)
    col = lambda i: (0, i)
    in_specs = [
        pl.BlockSpec((tm, d), row),
        pl.BlockSpec((1, d), const),
        pl.BlockSpec(wn.shape, const),
        pl.BlockSpec(wg.shape, const),
        pl.BlockSpec(wt.shape, const),
    ]
    out_shape = (
        jax.ShapeDtypeStruct((n, WIDTH), BF16),
        jax.ShapeDtypeStruct((n, HEAD_DIM), BF16),
        jax.ShapeDtypeStruct((n, HEAD_DIM), BF16),
        jax.ShapeDtypeStruct((n, n_gate), BF16),
        jax.ShapeDtypeStruct((WIDTH, n), BF16),
        jax.ShapeDtypeStruct((n // BAND_TQ, WIDTH, BAND_TQ), BF16),
        jax.ShapeDtypeStruct((WIDTH, n), BF16),
        jax.ShapeDtypeStruct((WIDTH, n), BF16),
        jax.ShapeDtypeStruct((n // DSA_T, HEAD_DIM, DSA_T), BF16),
        jax.ShapeDtypeStruct((HEADS, n), F32),
    )
    out_specs = (
        pl.BlockSpec((tm, WIDTH), row),
        pl.BlockSpec((tm, HEAD_DIM), row),
        pl.BlockSpec((tm, HEAD_DIM), row),
        pl.BlockSpec((tm, n_gate), row),
        pl.BlockSpec((WIDTH, tm), col),
        pl.BlockSpec((tm // BAND_TQ, WIDTH, BAND_TQ), lambda i: (i, 0, 0)),
        pl.BlockSpec((WIDTH, tm), col),
        pl.BlockSpec((WIDTH, tm), col),
        pl.BlockSpec((tm // DSA_T, HEAD_DIM, DSA_T), lambda i: (i, 0, 0)),
        pl.BlockSpec((HEADS, tm), col),
    )
    blk = 2 * (_nbytes((tm, d), F32) + _nbytes(wn.shape, BF16) + _nbytes(wg.shape, BF16)
               + _nbytes(wt.shape, BF16) + _nbytes((tm, WIDTH), BF16)
               + 2 * _nbytes((tm, V7X_LANES), BF16) + _nbytes((tm, n_gate), BF16)
               + 4 * _nbytes((WIDTH, tm), BF16) + _nbytes((HEAD_DIM, tm), BF16)
               + _nbytes((HEADS, tm), F32))
    return pl.pallas_call(
        _proj_kernel, out_shape=out_shape, grid=grid, in_specs=in_specs, out_specs=out_specs,
        name="proj",
        compiler_params=pltpu.CompilerParams(
            dimension_semantics=("arbitrary",), vmem_limit_bytes=_vmem_limit(blk)),
    )(x2, gain, wn, wg, wt)


def _band_kernel(qT_ref, k_ref, vT_ref, bias_ref, o_ref, s_ref, p_ref):
    t = pl.program_id(1)
    row = lax.broadcasted_iota(jnp.int32, (V7X_LANES, BAND_TQ), 0)
    first_half = row < HEAD_DIM
    blocks = []
    for i in range(BAND_NBLK):
        kidx = t - (BAND_NBLK - 1) + i
        kc = jnp.maximum(kidx, 0)
        blocks.append((kidx, kc, pl.multiple_of(kc * BAND_TQ, BAND_TQ)))

    m = []
    for h in range(HEADS):
        cs = slice(V7X_LANES * (h // 2), V7X_LANES * (h // 2 + 1))
        qT = qT_ref[cs, :]
        qe = jnp.where(first_half if h % 2 == 0 else jnp.logical_not(first_half), qT,
                       jnp.zeros_like(qT))
        m_el = None
        for i, (kidx, kc, ks) in enumerate(blocks):
            ksl = slice(BAND_TQ * i, BAND_TQ * (i + 1))
            s = jnp.dot(k_ref[pl.ds(ks, BAND_TQ), cs], qe, preferred_element_type=F32)
            s = s + bias_ref[h, ksl, :]
            if i < BAND_NBLK - 1:
                s = jnp.where(kidx >= 0, s, NEG)
            s_ref[h, ksl, :] = s
            m_el = s if m_el is None else jnp.maximum(m_el, s)
        m.append(jnp.max(m_el, axis=0, keepdims=True))

    l = []
    for h in range(HEADS):
        pr = jnp.exp2(s_ref[h] - m[h])
        l.append(jnp.sum(pr, axis=0, keepdims=True))
        p_ref[h] = pr.astype(BF16)

    outs = []
    for h in range(HEADS):
        acc = None
        for i, (kidx, kc, ks) in enumerate(blocks):
            vT = vT_ref[kc, HEAD_DIM * h:HEAD_DIM * (h + 1), :]
            pv = jnp.dot(vT, p_ref[h, BAND_TQ * i:BAND_TQ * (i + 1), :],
                         preferred_element_type=F32)
            acc = pv if acc is None else acc + pv
        outs.append(acc * (1.0 / l[h]))
    o_ref[...] = jnp.concatenate(outs, axis=0).T


def _band(qaT, ka3, vaT4, abiasT):
    b, s, _ = ka3.shape
    nt = s // BAND_TQ
    grid = (b, nt)
    in_specs = [
        pl.BlockSpec((WIDTH, BAND_TQ), lambda bi, t: (0, bi * nt + t)),
        pl.BlockSpec((None, s, WIDTH), lambda bi, t: (bi, 0, 0)),
        pl.BlockSpec((None, nt, WIDTH, BAND_TQ), lambda bi, t: (bi, 0, 0, 0)),
        pl.BlockSpec(abiasT.shape, lambda bi, t: (0, 0, 0)),
    ]
    out_specs = pl.BlockSpec((None, BAND_TQ, WIDTH), lambda bi, t: (bi, t, 0))
    scratch = [
        pltpu.VMEM((HEADS, BAND_WIN, BAND_TQ), F32),
        pltpu.VMEM((HEADS, BAND_WIN, BAND_TQ), BF16),
    ]
    blk = (2 * (_nbytes((WIDTH, BAND_TQ), BF16) + 2 * _nbytes((s, WIDTH), BF16)
                + _nbytes(abiasT.shape, F32) + _nbytes((BAND_TQ, WIDTH), F32))
           + _nbytes((HEADS, BAND_WIN, BAND_TQ), F32) + _nbytes((HEADS, BAND_WIN, BAND_TQ), BF16))
    return pl.pallas_call(
        _band_kernel, out_shape=jax.ShapeDtypeStruct((b, s, WIDTH), F32),
        grid=grid, in_specs=in_specs, out_specs=out_specs, scratch_shapes=scratch, name="band",
        compiler_params=pltpu.CompilerParams(
            dimension_semantics=("arbitrary", "arbitrary"), vmem_limit_bytes=_vmem_limit(blk)),
    )(qaT, ka3, vaT4, abiasT)


def _make_dsa_kernel(n_sel):
    T = DSA_T
    kf = float(n_sel)
    idx_scale = (HEADS ** -0.5) * (HEAD_DIM ** -0.5)

    def kernel(qbT_ref, qiT_ref, wiT_ref, ki_ref, kb_ref, vbT_ref, gbias_ref, adm_ref, o_ref,
               sc_ref, m_ref, acc_ref, tmax_ref, s_ref, p_ref, vx_ref):
        j = pl.program_id(1)
        nt = j + 1
        zrow = jnp.zeros((1, T), F32)

        def tile_rows(kt):
            return pl.ds(pl.multiple_of(kt * T, T), T)

        wis = wiT_ref[...] * idx_scale

        def fold8(a):
            return jnp.sum(a.reshape(a.shape[0] // 8, 8, T), axis=0)

        HT = T // 2

        raw_ref = s_ref.at[0]

        def p1_half(kt_dot, kt_post, half, carry):
            hs = slice(half * HT, (half + 1) * HT)
            if kt_post is not None:
                mn, mx, cgt0, cge0 = carry
                acc = raw_ref[hs, :]
                mn = jnp.minimum(mn, jnp.min(acc, axis=0, keepdims=True))
                mx = jnp.maximum(mx, jnp.max(acc, axis=0, keepdims=True))
                slab = (kt_post == j).astype(jnp.int32) + 2 * (kt_post > j).astype(jnp.int32)
                sc = acc + adm_ref[slab, hs, :]
                sc_ref[pl.ds(pl.multiple_of(kt_post * T + half * HT, HT), HT), :] = sc
                cgt0 = cgt0 + fold8(jnp.where(sc > 0.0, 1.0, 0.0))
                cge0 = cge0 + fold8(jnp.where(sc >= 0.0, 1.0, 0.0))
                carry = (mn, mx, cgt0, cge0)
            if kt_dot is not None:
                kt_ld = jnp.minimum(kt_dot, pl.num_programs(1) - 1)
                ki_t = ki_ref[pl.ds(pl.multiple_of(kt_ld * T + half * HT, HT), HT), :]
                acc = jnp.zeros((HT, T), F32)
                for h in range(HEADS):
                    lg = jnp.dot(ki_t, qiT_ref[HEAD_DIM * h:HEAD_DIM * (h + 1), :],
                                 preferred_element_type=F32)
                    acc = acc + jnp.maximum(lg, 0.0) * wis[h:h + 1, :]
                raw_ref[hs, :] = acc
            return carry

        ntp = nt + nt % 2

        def p1_tile(kt_dot, kt_post, carry):
            for half in range(2):
                carry = p1_half(kt_dot, kt_post, half, carry)
            return carry

        def p1(i, carry):
            carry = p1_tile(2 * i + 1, 2 * i, carry)
            return p1_tile(2 * i + 2, 2 * i + 1, carry)

        z8 = jnp.zeros((8, T), F32)
        carry = (jnp.full((1, T), jnp.inf, F32), jnp.full((1, T), -jnp.inf, F32), z8, z8)
        p1_tile(0, None, None)
        carry = lax.fori_loop(0, ntp // 2 - 1, p1, carry)
        carry = p1_tile(ntp - 1, ntp - 2, carry)
        carry = p1_tile(None, ntp - 1, carry)
        lo0, hi0, cgt0, cge0 = carry
        cgt0 = jnp.sum(cgt0, axis=0, keepdims=True)
        cge0 = jnp.sum(cge0, axis=0, keepdims=True)

        qpos = j * T + lax.broadcasted_iota(jnp.int32, (1, T), 1)
        n_adm = ((qpos // CHUNK + 1) * CHUNK).astype(F32)
        small = n_adm <= kf
        above0 = cgt0 >= kf
        below0 = cge0 < kf
        zero_tie = jnp.logical_not(jnp.logical_or(above0, below0))
        lo_pos = lo0 > 0.0
        lo1 = jnp.where(zero_tie, 0.0, jnp.where(jnp.logical_and(above0, jnp.logical_not(lo_pos)),
                                                 0.0, lo0))
        cl1 = jnp.where(jnp.logical_or(zero_tie, jnp.logical_and(above0, jnp.logical_not(lo_pos))),
                        cge0, n_adm)
        hi1 = jnp.where(zero_tie, 0.0, jnp.where(below0, jnp.minimum(hi0, 0.0), hi0))

        def count_ge(th):
            def body(i, c8):
                for u in range(2):
                    blk = sc_ref[tile_rows(2 * i + u), :]
                    c8 = c8 + fold8(jnp.where(blk >= th, 1.0, 0.0))
                return c8
            return jnp.sum(lax.fori_loop(0, ntp // 2, body, z8), axis=0, keepdims=True)

        def any_open(done):
            return jnp.max(jnp.where(done, 0.0, 1.0)) > 0.0

        def settled(cl_):
            return jnp.logical_or(jnp.logical_or(small, zero_tie), cl_ == kf)

        def b_cond(st):
            it, _, _, _, open_ = st
            return jnp.logical_and(it < BISECT_MAX_ITERS, open_)

        def b_body(st):
            it, lo, hi, cl, _ = st
            mid = 0.5 * lo + 0.5 * hi
            c = count_ge(mid)
            ge = c >= kf
            lo = jnp.where(ge, mid, lo)
            cl = jnp.where(ge, c, cl)
            hi = jnp.where(ge, hi, mid)
            return it + 1, lo, hi, cl, any_open(settled(cl))

        _, lo, hi, cl, open_ = lax.while_loop(
            b_cond, b_body, (jnp.int32(0), lo1, hi1, cl1, any_open(settled(cl1))))
        has_ties = jnp.logical_or(open_, jnp.max(jnp.where(
            jnp.logical_and(zero_tie, jnp.logical_not(small)), 1.0, 0.0)) > 0.0)

        def fast_mask():
            thr = jnp.where(small, F32_MIN, lo)

            def body(kt, _):
                blk = sc_ref[tile_rows(kt), :]
                sc_ref[tile_rows(kt), :] = jnp.where(blk >= thr, 0.0, NEG)
                return 0
            lax.fori_loop(0, nt, body, 0)

        def exact_mask():
            def done_of(lo_, hi_, cl_):
                return jnp.logical_or(jnp.logical_or(small, cl_ == kf), lo_ >= hi_)

            def s_body(st):
                lo_, hi_, cl_, _ = st
                done = done_of(lo_, hi_, cl_)
                mid = 0.5 * lo_ + 0.5 * hi_
                mid = jnp.where(mid > lo_, mid, hi_)

                def body(kt, c3):
                    c, mn_ge, mx_lt = c3
                    blk = sc_ref[tile_rows(kt), :]
                    ge_ = blk >= mid
                    c = c + jnp.sum(jnp.where(ge_, 1.0, 0.0), axis=0, keepdims=True)
                    mn_ge = jnp.minimum(mn_ge, jnp.min(jnp.where(ge_, blk, jnp.inf),
                                                       axis=0, keepdims=True))
                    mx_lt = jnp.maximum(mx_lt, jnp.max(jnp.where(ge_, -jnp.inf, blk),
                                                       axis=0, keepdims=True))
                    return c, mn_ge, mx_lt
                c, mn_ge, mx_lt = lax.fori_loop(
                    0, nt, body, (zrow, jnp.full((1, T), jnp.inf, F32),
                                  jnp.full((1, T), -jnp.inf, F32)))
                ge = c >= kf
                lo_n = jnp.where(done, lo_, jnp.where(ge, mn_ge, lo_))
                cl_n = jnp.where(done, cl_, jnp.where(ge, c, cl_))
                hi_n = jnp.where(done, hi_, jnp.where(ge, hi_, mx_lt))
                return lo_n, hi_n, cl_n, any_open(done_of(lo_n, hi_n, cl_n))

            lo_e, _, cl_e, _ = lax.while_loop(
                lambda st: st[3], s_body, (lo, hi, cl, any_open(done_of(lo, hi, cl))))
            thr = jnp.where(small, F32_MIN, lo_e)

            def count_gt():
                def body(kt, c8):
                    blk = sc_ref[tile_rows(kt), :]
                    return c8 + fold8(jnp.where(blk > thr, 1.0, 0.0))
                return jnp.sum(lax.fori_loop(0, nt, body, z8), axis=0, keepdims=True)
            cgt = lax.cond(open_, count_gt, lambda: cgt0)
            need = jnp.where(jnp.logical_or(small, cl_e == kf), jnp.inf, kf - cgt)

            UNDECIDED = -1.0

            def m_body(kt, st):
                carry, xt, xc = st
                blk = sc_ref[tile_rows(kt), :]
                eq = jnp.where(blk == thr, 1.0, 0.0)
                cnt = jnp.sum(fold8(eq), axis=0, keepdims=True)
                has_budget = carry < need
                crossing = jnp.logical_and(has_budget, carry + cnt > need)
                tie_val = jnp.where(crossing, UNDECIDED, jnp.where(has_budget, 0.0, NEG))
                sc_ref[tile_rows(kt), :] = jnp.where(
                    blk > thr, 0.0, jnp.where(blk == thr, tie_val, NEG))
                xt = jnp.where(crossing, kt.astype(F32), xt)
                xc = jnp.where(crossing, carry, xc)
                return carry + cnt, xt, xc
            _, xt, xc = lax.fori_loop(0, nt, m_body, (zrow, jnp.full((1, T), -1.0, F32), zrow))

            def r_body(st):
                xt_, ktf = st
                kt = ktf.astype(jnp.int32)
                msk = sc_ref[tile_rows(kt), :]
                und = msk == UNDECIDED
                r_io = lax.broadcasted_iota(jnp.int32, (T, T), 0)
                c_io = lax.broadcasted_iota(jnp.int32, (T, T), 1)
                ltri = jnp.where(c_io < r_io, 1.0, 0.0).astype(BF16)
                rank = jnp.dot(ltri, jnp.where(und, 1.0, 0.0).astype(BF16),
                               preferred_element_type=F32) + xc
                sc_ref[tile_rows(kt), :] = jnp.where(
                    und, jnp.where(rank < need, 0.0, NEG), msk)
                xt_ = jnp.where(xt_ == ktf, -1.0, xt_)
                return xt_, jnp.max(xt_)
            lax.while_loop(lambda st: st[1] >= 0.0, r_body, (xt, jnp.max(xt)))

        lax.cond(has_ties, exact_mask, fast_mask)

        m_ref[...] = jnp.full(m_ref.shape, NEG, F32)
        acc_ref[...] = jnp.zeros(acc_ref.shape, F32)
        vx_ref[HEAD_DIM:, :] = jnp.ones((VX_ROWS - HEAD_DIM, T), BF16)

        def step(kt_a, bias_slab, kt_bc):
            if kt_bc is not None:
                vx_ref[0:HEAD_DIM, :] = vbT_ref[kt_bc]
                m_old = m_ref[...]
                m_new = jnp.maximum(m_old, tmax_ref[...])
                alpha = jnp.exp2(m_old - m_new)
                m_ref[...] = m_new
            if kt_a is not None:
                kb_t = kb_ref[tile_rows(kt_a), :]
                msk = sc_ref[tile_rows(kt_a), :]
            tmax = []
            for h in range(HEADS):
                rows = slice(HEAD_DIM * h, HEAD_DIM * (h + 1))
                if kt_bc is not None:
                    p_ref[h] = jnp.exp2(s_ref[h] - m_new[h:h + 1, :]).astype(BF16)
                if kt_a is not None:
                    add = msk if bias_slab is None else msk + gbias_ref[bias_slab, h]
                    s = jnp.dot(kb_t, qbT_ref[rows, :], preferred_element_type=F32) + add
                    s_ref[h] = s
                    tmax.append(jnp.max(s, axis=0, keepdims=True))
                if kt_bc is not None:
                    for hc in ([h - PV_LAG] if h < HEADS - 1 else range(h - PV_LAG, HEADS)):
                        if hc < 0:
                            continue
                        acc_ref[hc] = alpha[hc:hc + 1, :] * acc_ref[hc] + jnp.dot(
                            vx_ref[...], p_ref[hc], preferred_element_type=F32)
            if kt_a is not None:
                tmax_ref[...] = jnp.concatenate(tmax, axis=0)

        n_far = jnp.maximum(j - 1, 0)
        step(0, jnp.clip(2 - j, 0, 2), None)

        def far_step(a, _):
            step(a, None, a - 1)
            return 0

        def near_step(a, _):
            step(a, a - j + 2, a - 1)
            return 0

        lax.fori_loop(1, n_far, far_step, 0)
        lax.fori_loop(jnp.maximum(n_far, 1), nt, near_step, 0)
        step(None, None, nt - 1)

        outs = []
        for h in range(HEADS):
            a = acc_ref[h]
            outs.append(a[0:HEAD_DIM, :] * (1.0 / a[HEAD_DIM:HEAD_DIM + 1, :]))
        o_ref[...] = jnp.concatenate(outs, axis=0).T

    return kernel


def _dsa(qbT, qiT, wiT, ki3, kb3, vbT4, gbias, adm, n_sel):
    b, s, _ = ki3.shape
    T = DSA_T
    nq = s // T
    grid = (b, nq)
    in_specs = [
        pl.BlockSpec((WIDTH, T), lambda bi, j: (0, bi * nq + j)),
        pl.BlockSpec((WIDTH, T), lambda bi, j: (0, bi * nq + j)),
        pl.BlockSpec((HEADS, T), lambda bi, j: (0, bi * nq + j)),
        pl.BlockSpec((None, s, HEAD_DIM), lambda bi, j: (bi, 0, 0)),
        pl.BlockSpec((None, s, HEAD_DIM), lambda bi, j: (bi, 0, 0)),
        pl.BlockSpec((None, nq, HEAD_DIM, T), lambda bi, j: (bi, 0, 0, 0)),
        pl.BlockSpec(gbias.shape, lambda bi, j: (0, 0, 0, 0)),
        pl.BlockSpec(adm.shape, lambda bi, j: (0, 0, 0)),
    ]
    out_specs = pl.BlockSpec((None, T, WIDTH), lambda bi, j: (bi, j, 0))
    scratch = [
        pltpu.VMEM((s + T, T), F32),
        pltpu.VMEM((HEADS, T), F32),
        pltpu.VMEM((HEADS, VX_ROWS, T), F32),
        pltpu.VMEM((HEADS, T), F32),
        pltpu.VMEM((HEADS, T, T), F32),
        pltpu.VMEM((HEADS, T, T), BF16),
        pltpu.VMEM((VX_ROWS, T), BF16),
    ]
    blk = (2 * (2 * _nbytes((WIDTH, T), BF16) + _nbytes((HEADS, T), F32)
                + 2 * _nbytes((s, V7X_LANES), BF16) + _nbytes((HEAD_DIM, s), BF16)
                + _nbytes(gbias.shape, F32) + _nbytes(adm.shape, F32)
                + _nbytes((T, WIDTH), F32))
           + _nbytes((s + T, T), F32) + 2 * _nbytes((HEADS, T), F32)
           + _nbytes((HEADS, VX_ROWS, T), F32) + _nbytes((HEADS, T, T), F32)
           + _nbytes((HEADS, T, T), BF16) + _nbytes((VX_ROWS, T), BF16))
    return pl.pallas_call(
        _make_dsa_kernel(n_sel), out_shape=jax.ShapeDtypeStruct((b, s, WIDTH), F32),
        grid=grid, in_specs=in_specs, out_specs=out_specs, scratch_shapes=scratch, name="dsa",
        compiler_params=pltpu.CompilerParams(
            dimension_semantics=("arbitrary", "arbitrary"), vmem_limit_bytes=_vmem_limit(blk)),
    )(qbT, qiT, wiT, ki3, kb3, vbT4, gbias, adm)


def _merge_kernel(x_ref, ya_ref, yb_ref, gate_ref, wa_ref, wb_ref, wo_ref, fg_ref, o_ref):
    d = x_ref.shape[1]
    za = gate_ref[:, 0:WIDTH].astype(F32)
    zb = gate_ref[:, WIDTH:2 * WIDTH].astype(F32)
    ga = gate_ref[:, 2 * WIDTH:2 * WIDTH + d].astype(F32)
    gb = gate_ref[:, 2 * WIDTH + d:2 * WIDTH + 2 * d].astype(F32)
    ua = (ya_ref[...] * (za * _sigmoid(za))).astype(BF16)
    ub = (yb_ref[...] * (zb * _sigmoid(zb))).astype(BF16)
    pa = jnp.dot(ua, wa_ref[...], preferred_element_type=F32)
    pb = jnp.dot(ub, wb_ref[...], preferred_element_type=F32)
    merged = _sigmoid(ga) * pa + _sigmoid(gb) * pb
    h = x_ref[...] + jnp.dot(merged.astype(BF16), wo_ref[...], preferred_element_type=F32)
    ms = jnp.mean(h * h, axis=-1, keepdims=True)
    o_ref[...] = (h * lax.rsqrt(ms + EPS)) * fg_ref[...]


def _merge(x2, ya2, yb2, gates, wa, wb, wo, fg, tm):
    n, d = x2.shape
    n_gate = gates.shape[1]
    grid = (n // tm,)
    row = lambda i: (i, 0)
    const = lambda i: (0, 0)
    in_specs = [
        pl.BlockSpec((tm, d), row),
        pl.BlockSpec((tm, WIDTH), row),
        pl.BlockSpec((tm, WIDTH), row),
        pl.BlockSpec((tm, n_gate), row),
        pl.BlockSpec(wa.shape, const),
        pl.BlockSpec(wb.shape, const),
        pl.BlockSpec(wo.shape, const),
        pl.BlockSpec((1, d), const),
    ]
    blk = 2 * (2 * _nbytes((tm, d), F32) + 2 * _nbytes((tm, WIDTH), F32)
               + _nbytes((tm, n_gate), BF16) + _nbytes(wa.shape, BF16) + _nbytes(wb.shape, BF16)
               + _nbytes(wo.shape, BF16))
    return pl.pallas_call(
        _merge_kernel, out_shape=jax.ShapeDtypeStruct((n, d), F32),
        grid=grid, in_specs=in_specs, out_specs=pl.BlockSpec((tm, d), row), name="merge",
        compiler_params=pltpu.CompilerParams(
            dimension_semantics=("arbitrary",), vmem_limit_bytes=_vmem_limit(blk)),
    )(x2, ya2, yb2, gates, wa, wb, wo, fg)


def _toeplitz(g, nrows, ncols):
    period = nrows + ncols
    lead = g.shape[:-1]
    p = jnp.concatenate([g[..., ::-1], jnp.zeros(lead + (1,), g.dtype)], axis=-1)
    flat = jnp.tile(p, (1,) * len(lead) + (nrows,))[..., :nrows * (period - 1)]
    x = flat.reshape(lead + (nrows, period - 1))
    return x[..., nrows - 1:nrows - 1 + ncols]


def _band_bias(a_rel_bias):
    pad = A_LEFT_CHUNKS * CHUNK
    rel = np.arange(BAND_TQ + BAND_WIN - 1) - (BAND_WIN - 1) + pad
    idx = np.clip(rel, -A_REL_CLIP, A_REL_CLIP) + A_REL_CLIP
    lo, hi = int(idx[0]), int(idx.max())
    n_flat = int((idx == hi).sum()) - 1
    assert np.array_equal(idx, np.minimum(np.arange(lo, lo + idx.size), hi))
    ab = a_rel_bias.astype(F32)
    g = jnp.concatenate([ab[:, lo:hi + 1], jnp.broadcast_to(ab[:, hi:hi + 1], (HEADS, n_flat))],
                        axis=1)
    bias = _toeplitz(g, BAND_TQ, BAND_WIN)
    qq = np.arange(BAND_TQ)[:, None]
    jj = np.arange(BAND_WIN)[None, :]
    in_band = np.logical_and(jj // CHUNK >= qq // CHUNK, jj // CHUNK <= qq // CHUNK + A_LEFT_CHUNKS)
    bias = jnp.where(jnp.asarray(in_band)[None], bias * LOG2E, NEG)
    return jnp.swapaxes(bias, 1, 2)


def _dsa_bias(t5_bias):
    T = DSA_T
    far = T5_BUCKETS // 2 - 1
    tb = t5_bias.astype(F32)
    slabs = [jnp.zeros((HEADS, T, T), F32)]
    for off in (-T, 0):
        rel = jnp.arange(2 * T - 1, dtype=jnp.int32) - (T - 1) + off
        g = (tb[_t5_bucket(rel)] - tb[far]) * LOG2E
        slabs.append(_toeplitz(g.T, T, T))
    return jnp.stack(slabs)


def _far_bucket_is_constant(s):
    half = T5_BUCKETS // 2
    max_exact = half // 2
    for n in (DSA_T + 1, max(s - 1, DSA_T + 1)):
        large = max_exact + int(math.log(n / max_exact) / math.log(T5_MAX_DIST / max_exact)
                                * (half - max_exact))
        if min(large, half - 1) != half - 1:
            return False
    return True


def kernel(x, norm_gain, w_in, a_rel_bias, t5_bias, w_a_out, w_b_out, w_out, final_gain):
    b, s, d = x.shape
    depth = w_in.shape[0]
    n = b * s
    assert s % DSA_T == 0 and s % BAND_TQ == 0
    assert _far_bucket_is_constant(s)
    tm = 512 if n % 512 == 0 else DSA_T
    n_sel = min(TOPK_MAX, s // 4)

    splits = (WIDTH,) * 4 + (WIDTH, HEAD_DIM, HEAD_DIM, WIDTH) + (WIDTH, HEAD_DIM, HEADS) + (d, d)
    offs = np.concatenate([[0], np.cumsum(splits)])
    assert offs[-1] == w_in.shape[2]

    abias = _band_bias
    gbias = _dsa_bias(t5_bias)
    kk = jnp.arange(DSA_T)[:, None] // CHUNK
    qq = jnp.arange(DSA_T)[None, :] // CHUNK
    adm = jnp.stack([jnp.zeros((DSA_T, DSA_T), F32),
                     jnp.where(kk <= qq, 0.0, -jnp.inf).astype(F32),
                     jnp.full((DSA_T, DSA_T), -jnp.inf, F32)])

    h2 = x.reshape(n, d)
    for l in range(depth):
        w = w_in[l]
        (qa, ka, va, za, qb, kb, vb, zb, qi, ki, wi, ga, gb) = [
            w[:, offs[i]:offs[i + 1]] for i in range(len(splits))]
        wn = jnp.concatenate([ka, ki, kb], axis=1).astype(BF16)
        wg = jnp.concatenate([za, zb, ga, gb], axis=1).astype(BF16)
        qscale = HEAD_DIM ** -0.5 * LOG2E
        wt = jnp.concatenate([qa * qscale, va, qb * qscale, qi, vb, wi,
                              jnp.zeros((d, HEADS), w.dtype)], axis=1).T.astype(BF16)

        ka_o, ki_o, kb_o, gates, qaT, vaT, qbT, qiT, vbT, wiT = _proj(
            h2, norm_gain[l].reshape(1, d).astype(F32), wn, wg, wt, tm)

        ya = _band(qaT, ka_o.reshape(b, s, WIDTH),
                   vaT.reshape(b, s // BAND_TQ, WIDTH, BAND_TQ), abias(a_rel_bias[l]))
        yb = _dsa(qbT, qiT, wiT, ki_o.reshape(b, s, HEAD_DIM), kb_o.reshape(b, s, HEAD_DIM),
                  vbT.reshape(b, s // DSA_T, HEAD_DIM, DSA_T), gbias, adm, n_sel)

        assert depth == 1
        h2 = _merge(h2, ya.reshape(n, WIDTH), yb.reshape(n, WIDTH), gates,
                    w_a_out[l].astype(BF16), w_b_out[l].astype(BF16), w_out[l].astype(BF16),
                    final_gain.reshape(1, d).astype(F32), tm)
    return h2.reshape(b, s, d)
```

```python
import math

import numpy as np
import jax
import jax.numpy as jnp
from jax import lax
from jax.experimental import pallas as pl
from jax.experimental.pallas import tpu as pltpu

F32 = jnp.float32
BF16 = jnp.bfloat16

CHUNK = 64
EPS = 1e-6
HEADS = 8
HEAD_DIM = 64
WIDTH = HEADS * HEAD_DIM
A_LEFT_CHUNKS = 8
A_REL_CLIP = 256
TOPK_MAX = 256
T5_BUCKETS = 32
T5_MAX_DIST = 128
NEG = -1e30
F32_MIN = float(np.finfo(np.float32).min)
LOG2E = math.log2(math.e)

V7X_VMEM_BYTES = 64 * 1024 * 1024
V7X_LANES = 128
VMEM_CAP_BYTES = 60000 * 1024
COMPILER_TEMP_BYTES = 8 * 1024 * 1024

BAND_TQ = 4 * CHUNK
BAND_WIN = A_LEFT_CHUNKS * CHUNK + BAND_TQ
BAND_KB = 128
BAND_NBLK = BAND_WIN // BAND_KB
DSA_T = 256
VX_ROWS = HEAD_DIM + 16
PV_LAG = 2
BISECT_MAX_ITERS = 26


def _vmem_limit(block_bytes):
    return int(min(block_bytes + COMPILER_TEMP_BYTES, VMEM_CAP_BYTES))


def _nbytes(shape, dtype):
    return int(np.prod(shape)) * jnp.dtype(dtype).itemsize


def _t5_bucket(rel):
    half = T5_BUCKETS // 2
    max_exact = half // 2
    ret = jnp.where(rel > 0, half, 0)
    n = jnp.abs(rel)
    nf = jnp.maximum(n, 1).astype(F32)
    large = max_exact + (jnp.log(nf / max_exact) / math.log(T5_MAX_DIST / max_exact)
                         * (half - max_exact)).astype(jnp.int32)
    large = jnp.minimum(large, half - 1)
    return ret + jnp.where(n < max_exact, n, large)


def _sigmoid(v):
    return 1.0 / (1.0 + jnp.exp(-v))


def _proj_kernel(x_ref, g_ref, wn_ref, wg_ref, wt_ref,
                 ka_ref, ki_ref, kb_ref, gate_ref,
                 qaT_ref, vaT_ref, qbT_ref, qiT_ref, vbT_ref, wiT_ref):
    xf = x_ref[...]
    ms = jnp.mean(xf * xf, axis=-1, keepdims=True)
    hn = ((xf * lax.rsqrt(ms + EPS)) * g_ref[...]).astype(BF16)
    tm = hn.shape[0]

    ka_ref[...] = jnp.dot(hn, wn_ref[:, 0:WIDTH], preferred_element_type=F32).astype(BF16)
    kk = jnp.dot(hn, wn_ref[:, WIDTH:WIDTH + 2 * HEAD_DIM], preferred_element_type=F32)
    ki_ref[...] = kk[:, :HEAD_DIM].astype(BF16)
    kb_ref[...] = kk[:, HEAD_DIM:].astype(BF16)
    n_gate = gate_ref.shape[1]
    for c0 in range(0, n_gate, WIDTH):
        gate_ref[:, c0:c0 + WIDTH] = jnp.dot(
            hn, wg_ref[:, c0:c0 + WIDTH], preferred_element_type=F32).astype(BF16)

    def t_rows(r0, nrows):
        return lax.dot_general(wt_ref[r0:r0 + nrows, :], hn, (((1,), (1,)), ((), ())),
                               preferred_element_type=F32)
    qaT_ref[...] = t_rows(0, WIDTH).astype(BF16)
    vaT = t_rows(WIDTH, WIDTH).astype(BF16)
    for s in range(tm // BAND_KB):
        vaT_ref[s] = vaT[:, s * BAND_KB:(s + 1) * BAND_KB]
    qbT_ref[...] = t_rows(2 * WIDTH, WIDTH).astype(BF16)
    qiT_ref[...] = t_rows(3 * WIDTH, WIDTH).astype(BF16)
    tail = t_rows(4 * WIDTH, HEAD_DIM + 2 * HEADS)
    for s in range(tm // DSA_T):
        vbT_ref[s] = tail[0:HEAD_DIM, s * DSA_T:(s + 1) * DSA_T].astype(BF16)
    wiT_ref[...] = tail[HEAD_DIM:HEAD_DIM + HEADS, :]


def _proj(x2, gain, wn, wg, wt, tm):
    n, d = x2.shape
    n_gate = wg.shape[1]
    grid = (n // tm,)
    const = lambda i: (0, 0)
    row = lambda i: (i, 0)
    col = lambda i: (0, i)
    in_specs = [
        pl.BlockSpec((tm, d), row),
        pl.BlockSpec((1, d), const),
        pl.BlockSpec(wn.shape, const),
        pl.BlockSpec(wg.shape, const),
        pl.BlockSpec(wt.shape, const),
    ]
    out_shape = (
        jax.ShapeDtypeStruct((n, WIDTH), BF16),
        jax.ShapeDtypeStruct((n, HEAD_DIM), BF16),
        jax.ShapeDtypeStruct((n, HEAD_DIM), BF16),
        jax.ShapeDtypeStruct((n, n_gate), BF16),
        jax.ShapeDtypeStruct((WIDTH, n), BF16),
        jax.ShapeDtypeStruct((n // BAND_KB, WIDTH, BAND_KB), BF16),
        jax.ShapeDtypeStruct((WIDTH, n), BF16),
        jax.ShapeDtypeStruct((WIDTH, n), BF16),
        jax.ShapeDtypeStruct((n // DSA_T, HEAD_DIM, DSA_T), BF16),
        jax.ShapeDtypeStruct((HEADS, n), F32),
    )
    out_specs = (
        pl.BlockSpec((tm, WIDTH), row),
        pl.BlockSpec((tm, HEAD_DIM), row),
        pl.BlockSpec((tm, HEAD_DIM), row),
        pl.BlockSpec((tm, n_gate), row),
        pl.BlockSpec((WIDTH, tm), col),
        pl.BlockSpec((tm // BAND_KB, WIDTH, BAND_KB), lambda i: (i, 0, 0)),
        pl.BlockSpec((WIDTH, tm), col),
        pl.BlockSpec((WIDTH, tm), col),
        pl.BlockSpec((tm // DSA_T, HEAD_DIM, DSA_T), lambda i: (i, 0, 0)),
        pl.BlockSpec((HEADS, tm), col),
    )
    blk = 2 * (_nbytes((tm, d), F32) + _nbytes(wn.shape, BF16) + _nbytes(wg.shape, BF16)
               + _nbytes(wt.shape, BF16) + _nbytes((tm, WIDTH), BF16)
               + 2 * _nbytes((tm, V7X_LANES), BF16) + _nbytes((tm, n_gate), BF16)
               + 4 * _nbytes((WIDTH, tm), BF16) + _nbytes((HEAD_DIM, tm), BF16)
               + _nbytes((HEADS, tm), F32))
    return pl.pallas_call(
        _proj_kernel, out_shape=out_shape, grid=grid, in_specs=in_specs, out_specs=out_specs,
        name="proj",
        compiler_params=pltpu.CompilerParams(
            dimension_semantics=("arbitrary",), vmem_limit_bytes=_vmem_limit(blk)),
    )(x2, gain, wn, wg, wt)


def _band_kernel(qT_ref, k_ref, vT_ref, bias_ref, o_ref, s_ref, p_ref):
    t = pl.program_id(1)
    row = lax.broadcasted_iota(jnp.int32, (V7X_LANES, BAND_TQ), 0)
    first_half = row < HEAD_DIM
    n_left = A_LEFT_CHUNKS * CHUNK // BAND_KB
    blocks = []
    for i in range(BAND_NBLK):
        kidx = t * (BAND_TQ // BAND_KB) - n_left + i
        kc = jnp.maximum(kidx, 0)
        blocks.append((kidx, kc, pl.multiple_of(kc * BAND_KB, BAND_KB)))

    def stage_a(h):
        cs = slice(V7X_LANES * (h // 2), V7X_LANES * (h // 2 + 1))
        qT = qT_ref[cs, :]
        qe = jnp.where(first_half if h % 2 == 0 else jnp.logical_not(first_half), qT,
                       jnp.zeros_like(qT))
        m_el = None
        for i, (kidx, kc, ks) in enumerate(blocks):
            ksl = slice(BAND_KB * i, BAND_KB * (i + 1))
            s = jnp.dot(k_ref[pl.ds(ks, BAND_KB), cs], qe, preferred_element_type=F32)
            s = s + bias_ref[h, ksl, :]
            if i < n_left:
                s = jnp.where(kidx >= 0, s, NEG)
            s_ref[h, ksl, :] = s
            m_el = s if m_el is None else jnp.maximum(m_el, s)
        return jnp.max(m_el, axis=0, keepdims=True)

    ones = jnp.ones((VX_ROWS - HEAD_DIM, BAND_KB), BF16)

    def stage_bc(h, m_h):
        p_ref[h] = jnp.exp2(s_ref[h] - m_h).astype(BF16)
        acc = None
        for i, (kidx, kc, ks) in enumerate(blocks):
            vx = jnp.concatenate([vT_ref[kc, HEAD_DIM * h:HEAD_DIM * (h + 1), :], ones], axis=0)
            pv = jnp.dot(vx, p_ref[h, BAND_KB * i:BAND_KB * (i + 1), :],
                         preferred_element_type=F32)
            acc = pv if acc is None else acc + pv
        return acc[0:HEAD_DIM, :] * (1.0 / acc[HEAD_DIM:HEAD_DIM + 1, :])

    m = [stage_a(h) for h in range(HEADS)]
    outs = [stage_bc(h, m[h]) for h in range(HEADS)]
    o_ref[...] = jnp.concatenate(outs, axis=0).T


def _band(qaT, ka3, vaT4, abiasT):
    b, s, _ = ka3.shape
    nt = s // BAND_TQ
    grid = (b, nt)
    once = pl.Buffered(1)
    in_specs = [
        pl.BlockSpec((WIDTH, BAND_TQ), lambda bi, t: (0, bi * nt + t)),
        pl.BlockSpec((None, s, WIDTH), lambda bi, t: (bi, 0, 0), pipeline_mode=once),
        pl.BlockSpec((None, s // BAND_KB, WIDTH, BAND_KB), lambda bi, t: (bi, 0, 0, 0),
                     pipeline_mode=once),
        pl.BlockSpec(abiasT.shape, lambda bi, t: (0, 0, 0), pipeline_mode=once),
    ]
    out_specs = pl.BlockSpec((None, BAND_TQ, WIDTH), lambda bi, t: (bi, t, 0))
    scratch = [
        pltpu.VMEM((HEADS, BAND_WIN, BAND_TQ), F32),
        pltpu.VMEM((HEADS, BAND_WIN, BAND_TQ), BF16),
    ]
    blk = (2 * (_nbytes((WIDTH, BAND_TQ), BF16) + _nbytes((BAND_TQ, WIDTH), F32))
           + 2 * _nbytes((s, WIDTH), BF16) + _nbytes(abiasT.shape, F32)
           + _nbytes((HEADS, BAND_WIN, BAND_TQ), F32) + _nbytes((HEADS, BAND_WIN, BAND_TQ), BF16))
    return pl.pallas_call(
        _band_kernel, out_shape=jax.ShapeDtypeStruct((b, s, WIDTH), F32),
        grid=grid, in_specs=in_specs, out_specs=out_specs, scratch_shapes=scratch, name="band",
        compiler_params=pltpu.CompilerParams(
            dimension_semantics=("arbitrary", "arbitrary"), vmem_limit_bytes=_vmem_limit(blk)),
    )(qaT, ka3, vaT4, abiasT)


def _make_dsa_kernel(n_sel):
    T = DSA_T
    kf = float(n_sel)
    idx_scale = (HEADS ** -0.5) * (HEAD_DIM ** -0.5)

    def kernel(qbT_ref, qiT_ref, wiT_ref, ki_ref, kb_ref, vbT_ref, gbias_ref, adm_ref, o_ref,
               sc_ref, m_ref, acc_ref, tmax_ref, s_ref, p_ref, vx_ref):
        j = pl.program_id(1)
        nt = j + 1
        zrow = jnp.zeros((1, T), F32)

        def tile_rows(kt):
            return pl.ds(pl.multiple_of(kt * T, T), T)

        wis = wiT_ref[...] * idx_scale

        def fold8(a):
            return jnp.sum(a.reshape(a.shape[0] // 8, 8, T), axis=0)

        HT = T // 2

        raw_ref = s_ref.at[0]

        def p1_half(kt_dot, kt_post, half, carry):
            hs = slice(half * HT, (half + 1) * HT)
            if kt_post is not None:
                mn, mx, cgt0, cge0 = carry
                acc = raw_ref[hs, :]
                mn = jnp.minimum(mn, jnp.min(acc, axis=0, keepdims=True))
                mx = jnp.maximum(mx, jnp.max(acc, axis=0, keepdims=True))
                slab = (kt_post == j).astype(jnp.int32) + 2 * (kt_post > j).astype(jnp.int32)
                sc = acc + adm_ref[slab, hs, :]
                sc_ref[pl.ds(pl.multiple_of(kt_post * T + half * HT, HT), HT), :] = sc
                cgt0 = cgt0 + fold8(jnp.where(sc > 0.0, 1.0, 0.0))
                cge0 = cge0 + fold8(jnp.where(sc >= 0.0, 1.0, 0.0))
                carry = (mn, mx, cgt0, cge0)
            if kt_dot is not None:
                kt_ld = jnp.minimum(kt_dot, pl.num_programs(1) - 1)
                ki_t = ki_ref[pl.ds(pl.multiple_of(kt_ld * T + half * HT, HT), HT), :]
                acc = jnp.zeros((HT, T), F32)
                for h in range(HEADS):
                    lg = jnp.dot(ki_t, qiT_ref[HEAD_DIM * h:HEAD_DIM * (h + 1), :],
                                 preferred_element_type=F32)
                    acc = acc + jnp.maximum(lg, 0.0) * wis[h:h + 1, :]
                raw_ref[hs, :] = acc
            return carry

        ntp = nt + nt % 2

        def p1_tile(kt_dot, kt_post, carry):
            for half in range(2):
                carry = p1_half(kt_dot, kt_post, half, carry)
            return carry

        def p1(i, carry):
            carry = p1_tile(2 * i + 1, 2 * i, carry)
            return p1_tile(2 * i + 2, 2 * i + 1, carry)

        z8 = jnp.zeros((8, T), F32)
        carry = (jnp.full((1, T), jnp.inf, F32), jnp.full((1, T), -jnp.inf, F32), z8, z8)
        p1_tile(0, None, None)
        carry = lax.fori_loop(0, ntp // 2 - 1, p1, carry)
        carry = p1_tile(ntp - 1, ntp - 2, carry)
        carry = p1_tile(None, ntp - 1, carry)
        lo0, hi0, cgt0, cge0 = carry
        cgt0 = jnp.sum(cgt0, axis=0, keepdims=True)
        cge0 = jnp.sum(cge0, axis=0, keepdims=True)

        qpos = j * T + lax.broadcasted_iota(jnp.int32, (1, T), 1)
        n_adm = ((qpos // CHUNK + 1) * CHUNK).astype(F32)
        small = n_adm <= kf
        above0 = cgt0 >= kf
        below0 = cge0 < kf
        zero_tie = jnp.logical_not(jnp.logical_or(above0, below0))
        lo_pos = lo0 > 0.0
        lo1 = jnp.where(zero_tie, 0.0, jnp.where(jnp.logical_and(above0, jnp.logical_not(lo_pos)),
                                                 0.0, lo0))
        cl1 = jnp.where(jnp.logical_or(zero_tie, jnp.logical_and(above0, jnp.logical_not(lo_pos))),
                        cge0, n_adm)
        hi1 = jnp.where(zero_tie, 0.0, jnp.where(below0, jnp.minimum(hi0, 0.0), hi0))

        def count_ge(th):
            def body(i, c8):
                for u in range(2):
                    blk = sc_ref[tile_rows(2 * i + u), :]
                    c8 = c8 + fold8(jnp.where(blk >= th, 1.0, 0.0))
                return c8
            return jnp.sum(lax.fori_loop(0, ntp // 2, body, z8), axis=0, keepdims=True)

        def any_open(done):
            return jnp.max(jnp.where(done, 0.0, 1.0)) > 0.0

        def settled(cl_):
            return jnp.logical_or(jnp.logical_or(small, zero_tie), cl_ == kf)

        def b_cond(st):
            it, _, _, _, open_ = st
            return jnp.logical_and(it < BISECT_MAX_ITERS, open_)

        def b_body(st):
            it, lo, hi, cl, _ = st
            mid = 0.5 * lo + 0.5 * hi
            c = count_ge(mid)
            ge = c >= kf
            lo = jnp.where(ge, mid, lo)
            cl = jnp.where(ge, c, cl)
            hi = jnp.where(ge, hi, mid)
            return it + 1, lo, hi, cl, any_open(settled(cl))

        _, lo, hi, cl, open_ = lax.while_loop(
            b_cond, b_body, (jnp.int32(0), lo1, hi1, cl1, any_open(settled(cl1))))
        has_ties = jnp.logical_or(open_, jnp.max(jnp.where(
            jnp.logical_and(zero_tie, jnp.logical_not(small)), 1.0, 0.0)) > 0.0)

        def fast_mask():
            thr = jnp.where(small, F32_MIN, lo)

            def body(kt, _):
                blk = sc_ref[tile_rows(kt), :]
                sc_ref[tile_rows(kt), :] = jnp.where(blk >= thr, 0.0, NEG)
                return 0
            lax.fori_loop(0, nt, body, 0)

        def exact_mask():
            def done_of(lo_, hi_, cl_):
                return jnp.logical_or(jnp.logical_or(small, cl_ == kf), lo_ >= hi_)

            def s_body(st):
                lo_, hi_, cl_, _ = st
                done = done_of(lo_, hi_, cl_)
                mid = 0.5 * lo_ + 0.5 * hi_
                mid = jnp.where(mid > lo_, mid, hi_)

                def body(kt, c3):
                    c, mn_ge, mx_lt = c3
                    blk = sc_ref[tile_rows(kt), :]
                    ge_ = blk >= mid
                    c = c + jnp.sum(jnp.where(ge_, 1.0, 0.0), axis=0, keepdims=True)
                    mn_ge = jnp.minimum(mn_ge, jnp.min(jnp.where(ge_, blk, jnp.inf),
                                                       axis=0, keepdims=True))
                    mx_lt = jnp.maximum(mx_lt, jnp.max(jnp.where(ge_, -jnp.inf, blk),
                                                       axis=0, keepdims=True))
                    return c, mn_ge, mx_lt
                c, mn_ge, mx_lt = lax.fori_loop(
                    0, nt, body, (zrow, jnp.full((1, T), jnp.inf, F32),
                                  jnp.full((1, T), -jnp.inf, F32)))
                ge = c >= kf
                lo_n = jnp.where(done, lo_, jnp.where(ge, mn_ge, lo_))
                cl_n = jnp.where(done, cl_, jnp.where(ge, c, cl_))
                hi_n = jnp.where(done, hi_, jnp.where(ge, hi_, mx_lt))
                return lo_n, hi_n, cl_n, any_open(done_of(lo_n, hi_n, cl_n))

            lo_e, _, cl_e, _ = lax.while_loop(
                lambda st: st[3], s_body, (lo, hi, cl, any_open(done_of(lo, hi, cl))))
            thr = jnp.where(small, F32_MIN, lo_e)

            def count_gt():
                def body(kt, c8):
                    blk = sc_ref[tile_rows(kt), :]
                    return c8 + fold8(jnp.where(blk > thr, 1.0, 0.0))
                return jnp.sum(lax.fori_loop(0, nt, body, z8), axis=0, keepdims=True)
            cgt = lax.cond(open_, count_gt, lambda: cgt0)
            need = jnp.where(jnp.logical_or(small, cl_e == kf), jnp.inf, kf - cgt)

            UNDECIDED = -1.0

            def m_body(kt, st):
                carry, xt, xc = st
                blk = sc_ref[tile_rows(kt), :]
                eq = jnp.where(blk == thr, 1.0, 0.0)
                cnt = jnp.sum(fold8(eq), axis=0, keepdims=True)
                has_budget = carry < need
                crossing = jnp.logical_and(has_budget, carry + cnt > need)
                tie_val = jnp.where(crossing, UNDECIDED, jnp.where(has_budget, 0.0, NEG))
                sc_ref[tile_rows(kt), :] = jnp.where(
                    blk > thr, 0.0, jnp.where(blk == thr, tie_val, NEG))
                xt = jnp.where(crossing, kt.astype(F32), xt)
                xc = jnp.where(crossing, carry, xc)
                return carry + cnt, xt, xc
            _, xt, xc = lax.fori_loop(0, nt, m_body, (zrow, jnp.full((1, T), -1.0, F32), zrow))

            def r_body(st):
                xt_, ktf = st
                kt = ktf.astype(jnp.int32)
                msk = sc_ref[tile_rows(kt), :]
                und = msk == UNDECIDED
                r_io = lax.broadcasted_iota(jnp.int32, (T, T), 0)
                c_io = lax.broadcasted_iota(jnp.int32, (T, T), 1)
                ltri = jnp.where(c_io < r_io, 1.0, 0.0).astype(BF16)
                rank = jnp.dot(ltri, jnp.where(und, 1.0, 0.0).astype(BF16),
                               preferred_element_type=F32) + xc
                sc_ref[tile_rows(kt), :] = jnp.where(
                    und, jnp.where(rank < need, 0.0, NEG), msk)
                xt_ = jnp.where(xt_ == ktf, -1.0, xt_)
                return xt_, jnp.max(xt_)
            lax.while_loop(lambda st: st[1] >= 0.0, r_body, (xt, jnp.max(xt)))

        lax.cond(has_ties, exact_mask, fast_mask)

        m_ref[...] = jnp.full(m_ref.shape, NEG, F32)
        acc_ref[...] = jnp.zeros(acc_ref.shape, F32)
        vx_ref[HEAD_DIM:, :] = jnp.ones((VX_ROWS - HEAD_DIM, T), BF16)

        def step(kt_a, bias_slab, kt_bc):
            if kt_bc is not None:
                vx_ref[0:HEAD_DIM, :] = vbT_ref[kt_bc]
                m_old = m_ref[...]
                m_new = jnp.maximum(m_old, tmax_ref[...])
                alpha = jnp.exp2(m_old - m_new)
                m_ref[...] = m_new
            if kt_a is not None:
                kb_t = kb_ref[tile_rows(kt_a), :]
                msk = sc_ref[tile_rows(kt_a), :]
            tmax = []
            for h in range(HEADS):
                rows = slice(HEAD_DIM * h, HEAD_DIM * (h + 1))
                if kt_bc is not None:
                    p_ref[h] = jnp.exp2(s_ref[h] - m_new[h:h + 1, :]).astype(BF16)
                if kt_a is not None:
                    add = msk if bias_slab is None else msk + gbias_ref[bias_slab, h]
                    s = jnp.dot(kb_t, qbT_ref[rows, :], preferred_element_type=F32) + add
                    s_ref[h] = s
                    tmax.append(jnp.max(s, axis=0, keepdims=True))
                if kt_bc is not None:
                    for hc in ([h - PV_LAG] if h < HEADS - 1 else range(h - PV_LAG, HEADS)):
                        if hc < 0:
                            continue
                        acc_ref[hc] = alpha[hc:hc + 1, :] * acc_ref[hc] + jnp.dot(
                            vx_ref[...], p_ref[hc], preferred_element_type=F32)
            if kt_a is not None:
                tmax_ref[...] = jnp.concatenate(tmax, axis=0)

        n_far = jnp.maximum(j - 1, 0)
        step(0, jnp.clip(2 - j, 0, 2), None)

        def far_step(a, _):
            step(a, None, a - 1)
            return 0

        def near_step(a, _):
            step(a, a - j + 2, a - 1)
            return 0

        lax.fori_loop(1, n_far, far_step, 0)
        lax.fori_loop(jnp.maximum(n_far, 1), nt, near_step, 0)
        step(None, None, nt - 1)

        outs = []
        for h in range(HEADS):
            a = acc_ref[h]
            outs.append(a[0:HEAD_DIM, :] * (1.0 / a[HEAD_DIM:HEAD_DIM + 1, :]))
        o_ref[...] = jnp.concatenate(outs, axis=0).T

    return kernel


def _dsa(qbT, qiT, wiT, ki3, kb3, vbT4, gbias, adm, n_sel):
    b, s, _ = ki3.shape
    T = DSA_T
    nq = s // T
    grid = (b, nq)
    in_specs = [
        pl.BlockSpec((WIDTH, T), lambda bi, j: (0, bi * nq + j)),
        pl.BlockSpec((WIDTH, T), lambda bi, j: (0, bi * nq + j)),
        pl.BlockSpec((HEADS, T), lambda bi, j: (0, bi * nq + j)),
        pl.BlockSpec((None, s, HEAD_DIM), lambda bi, j: (bi, 0, 0)),
        pl.BlockSpec((None, s, HEAD_DIM), lambda bi, j: (bi, 0, 0)),
        pl.BlockSpec((None, nq, HEAD_DIM, T), lambda bi, j: (bi, 0, 0, 0)),
        pl.BlockSpec(gbias.shape, lambda bi, j: (0, 0, 0, 0)),
        pl.BlockSpec(adm.shape, lambda bi, j: (0, 0, 0)),
    ]
    out_specs = pl.BlockSpec((None, T, WIDTH), lambda bi, j: (bi, j, 0))
    scratch = [
        pltpu.VMEM((s + T, T), F32),
        pltpu.VMEM((HEADS, T), F32),
        pltpu.VMEM((HEADS, VX_ROWS, T), F32),
        pltpu.VMEM((HEADS, T), F32),
        pltpu.VMEM((HEADS, T, T), F32),
        pltpu.VMEM((HEADS, T, T), BF16),
        pltpu.VMEM((VX_ROWS, T), BF16),
    ]
    blk = (2 * (2 * _nbytes((WIDTH, T), BF16) + _nbytes((HEADS, T), F32)
                + 2 * _nbytes((s, V7X_LANES), BF16) + _nbytes((HEAD_DIM, s), BF16)
                + _nbytes(gbias.shape, F32) + _nbytes(adm.shape, F32)
                + _nbytes((T, WIDTH), F32))
           + _nbytes((s + T, T), F32) + 2 * _nbytes((HEADS, T), F32)
           + _nbytes((HEADS, VX_ROWS, T), F32) + _nbytes((HEADS, T, T), F32)
           + _nbytes((HEADS, T, T), BF16) + _nbytes((VX_ROWS, T), BF16))
    return pl.pallas_call(
        _make_dsa_kernel(n_sel), out_shape=jax.ShapeDtypeStruct((b, s, WIDTH), F32),
        grid=grid, in_specs=in_specs, out_specs=out_specs, scratch_shapes=scratch, name="dsa",
        compiler_params=pltpu.CompilerParams(
            dimension_semantics=("arbitrary", "arbitrary"), vmem_limit_bytes=_vmem_limit(blk)),
    )(qbT, qiT, wiT, ki3, kb3, vbT4, gbias, adm)


def _merge_kernel(x_ref, ya_ref, yb_ref, gate_ref, wa_ref, wb_ref, wo_ref, fg_ref, o_ref):
    d = x_ref.shape[1]
    za = gate_ref[:, 0:WIDTH].astype(F32)
    zb = gate_ref[:, WIDTH:2 * WIDTH].astype(F32)
    ga = gate_ref[:, 2 * WIDTH:2 * WIDTH + d].astype(F32)
    gb = gate_ref[:, 2 * WIDTH + d:2 * WIDTH + 2 * d].astype(F32)
    ua = (ya_ref[...] * (za * _sigmoid(za))).astype(BF16)
    ub = (yb_ref[...] * (zb * _sigmoid(zb))).astype(BF16)
    pa = jnp.dot(ua, wa_ref[...], preferred_element_type=F32)
    pb = jnp.dot(ub, wb_ref[...], preferred_element_type=F32)
    merged = _sigmoid(ga) * pa + _sigmoid(gb) * pb
    h = x_ref[...] + jnp.dot(merged.astype(BF16), wo_ref[...], preferred_element_type=F32)
    ms = jnp.mean(h * h, axis=-1, keepdims=True)
    o_ref[...] = (h * lax.rsqrt(ms + EPS)) * fg_ref[...]


def _merge(x2, ya2, yb2, gates, wa, wb, wo, fg, tm):
    n, d = x2.shape
    n_gate = gates.shape[1]
    grid = (n // tm,)
    row = lambda i: (i, 0)
    const = lambda i: (0, 0)
    in_specs = [
        pl.BlockSpec((tm, d), row),
        pl.BlockSpec((tm, WIDTH), row),
        pl.BlockSpec((tm, WIDTH), row),
        pl.BlockSpec((tm, n_gate), row),
        pl.BlockSpec(wa.shape, const),
        pl.BlockSpec(wb.shape, const),
        pl.BlockSpec(wo.shape, const),
        pl.BlockSpec((1, d), const),
    ]
    blk = 2 * (2 * _nbytes((tm, d), F32) + 2 * _nbytes((tm, WIDTH), F32)
               + _nbytes((tm, n_gate), BF16) + _nbytes(wa.shape, BF16) + _nbytes(wb.shape, BF16)
               + _nbytes(wo.shape, BF16))
    return pl.pallas_call(
        _merge_kernel, out_shape=jax.ShapeDtypeStruct((n, d), F32),
        grid=grid, in_specs=in_specs, out_specs=pl.BlockSpec((tm, d), row), name="merge",
        compiler_params=pltpu.CompilerParams(
            dimension_semantics=("arbitrary",), vmem_limit_bytes=_vmem_limit(blk)),
    )(x2, ya2, yb2, gates, wa, wb, wo, fg)


def _toeplitz(g, nrows, ncols):
    period = nrows + ncols
    lead = g.shape[:-1]
    p = jnp.concatenate([g[..., ::-1], jnp.zeros(lead + (1,), g.dtype)], axis=-1)
    flat = jnp.tile(p, (1,) * len(lead) + (nrows,))[..., :nrows * (period - 1)]
    x = flat.reshape(lead + (nrows, period - 1))
    return x[..., nrows - 1:nrows - 1 + ncols]


def _band_bias(a_rel_bias):
    pad = A_LEFT_CHUNKS * CHUNK
    rel = np.arange(BAND_TQ + BAND_WIN - 1) - (BAND_WIN - 1) + pad
    idx = np.clip(rel, -A_REL_CLIP, A_REL_CLIP) + A_REL_CLIP
    lo, hi = int(idx[0]), int(idx.max())
    n_flat = int((idx == hi).sum()) - 1
    assert np.array_equal(idx, np.minimum(np.arange(lo, lo + idx.size), hi))
    ab = a_rel_bias.astype(F32)
    g = jnp.concatenate([ab[:, lo:hi + 1], jnp.broadcast_to(ab[:, hi:hi + 1], (HEADS, n_flat))],
                        axis=1)
    bias = _toeplitz(g, BAND_TQ, BAND_WIN)
    qq = np.arange(BAND_TQ)[:, None]
    jj = np.arange(BAND_WIN)[None, :]
    in_band = np.logical_and(jj // CHUNK >= qq // CHUNK, jj // CHUNK <= qq // CHUNK + A_LEFT_CHUNKS)
    bias = jnp.where(jnp.asarray(in_band)[None], bias * LOG2E, NEG)
    return jnp.swapaxes(bias, 1, 2)


def _dsa_bias(t5_bias):
    T = DSA_T
    far = T5_BUCKETS // 2 - 1
    tb = t5_bias.astype(F32)
    slabs = [jnp.zeros((HEADS, T, T), F32)]
    for off in (-T, 0):
        rel = jnp.arange(2 * T - 1, dtype=jnp.int32) - (T - 1) + off
        g = (tb[_t5_bucket(rel)] - tb[far]) * LOG2E
        slabs.append(_toeplitz(g.T, T, T))
    return jnp.stack(slabs)


def _far_bucket_is_constant(s):
    half = T5_BUCKETS // 2
    max_exact = half // 2
    for n in (DSA_T + 1, max(s - 1, DSA_T + 1)):
        large = max_exact + int(math.log(n / max_exact) / math.log(T5_MAX_DIST / max_exact)
                                * (half - max_exact))
        if min(large, half - 1) != half - 1:
            return False
    return True


def kernel(x, norm_gain, w_in, a_rel_bias, t5_bias, w_a_out, w_b_out, w_out, final_gain):
    b, s, d = x.shape
    depth = w_in.shape[0]
    n = b * s
    assert s % DSA_T == 0 and s % BAND_TQ == 0
    assert _far_bucket_is_constant(s)
    tm = 512 if n % 512 == 0 else DSA_T
    n_sel = min(TOPK_MAX, s // 4)

    splits = (WIDTH,) * 4 + (WIDTH, HEAD_DIM, HEAD_DIM, WIDTH) + (WIDTH, HEAD_DIM, HEADS) + (d, d)
    offs = np.concatenate([[0], np.cumsum(splits)])
    assert offs[-1] == w_in.shape[2]

    abias = _band_bias
    gbias = _dsa_bias(t5_bias)
    kk = jnp.arange(DSA_T)[:, None] // CHUNK
    qq = jnp.arange(DSA_T)[None, :] // CHUNK
    adm = jnp.stack([jnp.zeros((DSA_T, DSA_T), F32),
                     jnp.where(kk <= qq, 0.0, -jnp.inf).astype(F32),
                     jnp.full((DSA_T, DSA_T), -jnp.inf, F32)])

    h2 = x.reshape(n, d)
    for l in range(depth):
        w = w_in[l]
        (qa, ka, va, za, qb, kb, vb, zb, qi, ki, wi, ga, gb) = [
            w[:, offs[i]:offs[i + 1]] for i in range(len(splits))]
        wn = jnp.concatenate([ka, ki, kb], axis=1).astype(BF16)
        wg = jnp.concatenate([za, zb, ga, gb], axis=1).astype(BF16)
        qscale = HEAD_DIM ** -0.5 * LOG2E
        wt = jnp.concatenate([qa * qscale, va, qb * qscale, qi, vb, wi,
                              jnp.zeros((d, HEADS), w.dtype)], axis=1).T.astype(BF16)

        ka_o, ki_o, kb_o, gates, qaT, vaT, qbT, qiT, vbT, wiT = _proj(
            h2, norm_gain[l].reshape(1, d).astype(F32), wn, wg, wt, tm)

        ya = _band(qaT, ka_o.reshape(b, s, WIDTH),
                   vaT.reshape(b, s // BAND_KB, WIDTH, BAND_KB), abias(a_rel_bias[l]))
        yb = _dsa(qbT, qiT, wiT, ki_o.reshape(b, s, HEAD_DIM), kb_o.reshape(b, s, HEAD_DIM),
                  vbT.reshape(b, s // DSA_T, HEAD_DIM, DSA_T), gbias, adm, n_sel)

        assert depth == 1
        h2 = _merge(h2, ya.reshape(n, WIDTH), yb.reshape(n, WIDTH), gates,
                    w_a_out[l].astype(BF16), w_b_out[l].astype(BF16), w_out[l].astype(BF16),
                    final_gain.reshape(1, d).astype(F32), tm)
    return h2.reshape(b, s, d)
```

```python
import math

import numpy as np
import jax
import jax.numpy as jnp
from jax import lax
from jax.experimental import pallas as pl
from jax.experimental.pallas import tpu as pltpu

F32 = jnp.float32
BF16 = jnp.bfloat16

CHUNK = 64
EPS = 1e-6
HEADS = 8
HEAD_DIM = 64
WIDTH = HEADS * HEAD_DIM
A_LEFT_CHUNKS = 8
A_REL_CLIP = 256
TOPK_MAX = 256
T5_BUCKETS = 32
T5_MAX_DIST = 128
NEG = -1e30
F32_MIN = float(np.finfo(np.float32).min)
LOG2E = math.log2(math.e)

V7X_VMEM_BYTES = 64 * 1024 * 1024
V7X_LANES = 128
VMEM_CAP_BYTES = 60000 * 1024
COMPILER_TEMP_BYTES = 8 * 1024 * 1024

BAND_TQ = 4 * CHUNK
BAND_WIN = A_LEFT_CHUNKS * CHUNK + BAND_TQ
BAND_KB = 128
BAND_NBLK = BAND_WIN // BAND_KB
DSA_T = 256
VX_ROWS = HEAD_DIM + 16
PV_LAG = 2
BISECT_MAX_ITERS = 26
BISECT_BLIND_ITERS = 16


def _vmem_limit(block_bytes):
    return int(min(block_bytes + COMPILER_TEMP_BYTES, VMEM_CAP_BYTES))


def _nbytes(shape, dtype):
    return int(np.prod(shape)) * jnp.dtype(dtype).itemsize


def _t5_bucket(rel):
    half = T5_BUCKETS // 2
    max_exact = half // 2
    ret = jnp.where(rel > 0, half, 0)
    n = jnp.abs(rel)
    nf = jnp.maximum(n, 1).astype(F32)
    large = max_exact + (jnp.log(nf / max_exact) / math.log(T5_MAX_DIST / max_exact)
                         * (half - max_exact)).astype(jnp.int32)
    large = jnp.minimum(large, half - 1)
    return ret + jnp.where(n < max_exact, n, large)


def _sigmoid(v):
    return 1.0 / (1.0 + jnp.exp(-v))


def _proj_kernel(x_ref, g_ref, wn_ref, wg_ref, wt_ref,
                 ka_ref, ki_ref, kb_ref, gate_ref,
                 qaT_ref, vaT_ref, qbT_ref, qiT_ref, vbT_ref, wiT_ref):
    xf = x_ref[...]
    ms = jnp.mean(xf * xf, axis=-1, keepdims=True)
    hn = ((xf * lax.rsqrt(ms + EPS)) * g_ref[...]).astype(BF16)
    tm = hn.shape[0]

    ka_ref[...] = jnp.dot(hn, wn_ref[:, 0:WIDTH], preferred_element_type=F32).astype(BF16)
    kk = jnp.dot(hn, wn_ref[:, WIDTH:WIDTH + 2 * HEAD_DIM], preferred_element_type=F32)
    ki_ref[...] = kk[:, :HEAD_DIM].astype(BF16)
    kb_ref[...] = kk[:, HEAD_DIM:].astype(BF16)
    n_gate = gate_ref.shape[1]
    for c0 in range(0, n_gate, WIDTH):
        gate_ref[:, c0:c0 + WIDTH] = jnp.dot(
            hn, wg_ref[:, c0:c0 + WIDTH], preferred_element_type=F32).astype(BF16)

    def t_rows(r0, nrows):
        return lax.dot_general(wt_ref[r0:r0 + nrows, :], hn, (((1,), (1,)), ((), ())),
                               preferred_element_type=F32)
    qaT_ref[...] = t_rows(0, WIDTH).astype(BF16)
    vaT = t_rows(WIDTH, WIDTH).astype(BF16)
    for s in range(tm // BAND_KB):
        vaT_ref[s] = vaT[:, s * BAND_KB:(s + 1) * BAND_KB]
    qbT_ref[...] = t_rows(2 * WIDTH, WIDTH).astype(BF16)
    qiT_ref[...] = t_rows(3 * WIDTH, WIDTH).astype(BF16)
    tail = t_rows(4 * WIDTH, HEAD_DIM + 2 * HEADS)
    for s in range(tm // DSA_T):
        vbT_ref[s] = tail[0:HEAD_DIM, s * DSA_T:(s + 1) * DSA_T].astype(BF16)
    wiT_ref[...] = tail[HEAD_DIM:HEAD_DIM + HEADS, :]


def _proj(x2, gain, wn, wg, wt, tm):
    n, d = x2.shape
    n_gate = wg.shape[1]
    grid = (n // tm,)
    const = lambda i: (0, 0)
    row = lambda i: (i, 0)
    col = lambda i: (0, i)
    in_specs = [
        pl.BlockSpec((tm, d), row),
        pl.BlockSpec((1, d), const),
        pl.BlockSpec(wn.shape, const),
        pl.BlockSpec(wg.shape, const),
        pl.BlockSpec(wt.shape, const),
    ]
    out_shape = (
        jax.ShapeDtypeStruct((n, WIDTH), BF16),
        jax.ShapeDtypeStruct((n, HEAD_DIM), BF16),
        jax.ShapeDtypeStruct((n, HEAD_DIM), BF16),
        jax.ShapeDtypeStruct((n, n_gate), BF16),
        jax.ShapeDtypeStruct((WIDTH, n), BF16),
        jax.ShapeDtypeStruct((n // BAND_KB, WIDTH, BAND_KB), BF16),
        jax.ShapeDtypeStruct((WIDTH, n), BF16),
        jax.ShapeDtypeStruct((WIDTH, n), BF16),
        jax.ShapeDtypeStruct((n // DSA_T, HEAD_DIM, DSA_T), BF16),
        jax.ShapeDtypeStruct((HEADS, n), F32),
    )
    out_specs = (
        pl.BlockSpec((tm, WIDTH), row),
        pl.BlockSpec((tm, HEAD_DIM), row),
        pl.BlockSpec((tm, HEAD_DIM), row),
        pl.BlockSpec((tm, n_gate), row),
        pl.BlockSpec((WIDTH, tm), col),
        pl.BlockSpec((tm // BAND_KB, WIDTH, BAND_KB), lambda i: (i, 0, 0)),
        pl.BlockSpec((WIDTH, tm), col),
        pl.BlockSpec((WIDTH, tm), col),
        pl.BlockSpec((tm // DSA_T, HEAD_DIM, DSA_T), lambda i: (i, 0, 0)),
        pl.BlockSpec((HEADS, tm), col),
    )
    blk = 2 * (_nbytes((tm, d), F32) + _nbytes(wn.shape, BF16) + _nbytes(wg.shape, BF16)
               + _nbytes(wt.shape, BF16) + _nbytes((tm, WIDTH), BF16)
               + 2 * _nbytes((tm, V7X_LANES), BF16) + _nbytes((tm, n_gate), BF16)
               + 4 * _nbytes((WIDTH, tm), BF16) + _nbytes((HEAD_DIM, tm), BF16)
               + _nbytes((HEADS, tm), F32))
    return pl.pallas_call(
        _proj_kernel, out_shape=out_shape, grid=grid, in_specs=in_specs, out_specs=out_specs,
        name="proj",
        compiler_params=pltpu.CompilerParams(
            dimension_semantics=("arbitrary",), vmem_limit_bytes=_vmem_limit(blk)),
    )(x2, gain, wn, wg, wt)


def _band_kernel(qT_ref, k_ref, vT_ref, bias_ref, o_ref, s_ref, p_ref):
    t = pl.program_id(1)
    row = lax.broadcasted_iota(jnp.int32, (V7X_LANES, BAND_TQ), 0)
    first_half = row < HEAD_DIM
    n_left = A_LEFT_CHUNKS * CHUNK // BAND_KB
    blocks = []
    for i in range(BAND_NBLK):
        kidx = t * (BAND_TQ // BAND_KB) - n_left + i
        kc = jnp.maximum(kidx, 0)
        blocks.append((kidx, kc, pl.multiple_of(kc * BAND_KB, BAND_KB)))

    def stage_a(h):
        cs = slice(V7X_LANES * (h // 2), V7X_LANES * (h // 2 + 1))
        qT = qT_ref[cs, :]
        qe = jnp.where(first_half if h % 2 == 0 else jnp.logical_not(first_half), qT,
                       jnp.zeros_like(qT))
        m_el = None
        for i, (kidx, kc, ks) in enumerate(blocks):
            ksl = slice(BAND_KB * i, BAND_KB * (i + 1))
            s = jnp.dot(k_ref[pl.ds(ks, BAND_KB), cs], qe, preferred_element_type=F32)
            s = s + bias_ref[h, ksl, :]
            if i < n_left:
                s = jnp.where(kidx >= 0, s, NEG)
            s_ref[h, ksl, :] = s
            m_el = s if m_el is None else jnp.maximum(m_el, s)
        return jnp.max(m_el, axis=0, keepdims=True)

    ones = jnp.ones((VX_ROWS - HEAD_DIM, BAND_KB), BF16)

    def stage_bc(h, m_h):
        p_ref[h] = jnp.exp2(s_ref[h] - m_h).astype(BF16)
        acc = None
        for i, (kidx, kc, ks) in enumerate(blocks):
            vx = jnp.concatenate([vT_ref[kc, HEAD_DIM * h:HEAD_DIM * (h + 1), :], ones], axis=0)
            pv = jnp.dot(vx, p_ref[h, BAND_KB * i:BAND_KB * (i + 1), :],
                         preferred_element_type=F32)
            acc = pv if acc is None else acc + pv
        return acc[0:HEAD_DIM, :] * (1.0 / acc[HEAD_DIM:HEAD_DIM + 1, :])

    m = [stage_a(h) for h in range(HEADS)]
    outs = [stage_bc(h, m[h]) for h in range(HEADS)]
    o_ref[...] = jnp.concatenate(outs, axis=0).T


def _band(qaT, ka3, vaT4, abiasT):
    b, s, _ = ka3.shape
    nt = s // BAND_TQ
    grid = (b, nt)
    once = pl.Buffered(1)
    in_specs = [
        pl.BlockSpec((WIDTH, BAND_TQ), lambda bi, t: (0, bi * nt + t)),
        pl.BlockSpec((None, s, WIDTH), lambda bi, t: (bi, 0, 0), pipeline_mode=once),
        pl.BlockSpec((None, s // BAND_KB, WIDTH, BAND_KB), lambda bi, t: (bi, 0, 0, 0),
                     pipeline_mode=once),
        pl.BlockSpec(abiasT.shape, lambda bi, t: (0, 0, 0), pipeline_mode=once),
    ]
    out_specs = pl.BlockSpec((None, BAND_TQ, WIDTH), lambda bi, t: (bi, t, 0))
    scratch = [
        pltpu.VMEM((HEADS, BAND_WIN, BAND_TQ), F32),
        pltpu.VMEM((HEADS, BAND_WIN, BAND_TQ), BF16),
    ]
    blk = (2 * (_nbytes((WIDTH, BAND_TQ), BF16) + _nbytes((BAND_TQ, WIDTH), F32))
           + 2 * _nbytes((s, WIDTH), BF16) + _nbytes(abiasT.shape, F32)
           + _nbytes((HEADS, BAND_WIN, BAND_TQ), F32) + _nbytes((HEADS, BAND_WIN, BAND_TQ), BF16))
    return pl.pallas_call(
        _band_kernel, out_shape=jax.ShapeDtypeStruct((b, s, WIDTH), F32),
        grid=grid, in_specs=in_specs, out_specs=out_specs, scratch_shapes=scratch, name="band",
        compiler_params=pltpu.CompilerParams(
            dimension_semantics=("arbitrary", "arbitrary"), vmem_limit_bytes=_vmem_limit(blk)),
    )(qaT, ka3, vaT4, abiasT)


def _make_dsa_kernel(n_sel):
    T = DSA_T
    kf = float(n_sel)
    idx_scale = (HEADS ** -0.5) * (HEAD_DIM ** -0.5)

    def kernel(qbT_ref, qiT_ref, wiT_ref, ki_ref, kb_ref, vbT_ref, gbias_ref, adm_ref, o_ref,
               sc_ref, m_ref, acc_ref, tmax_ref, s_ref, p_ref, vx_ref):
        j = pl.program_id(1)
        nt = j + 1
        zrow = jnp.zeros((1, T), F32)

        def tile_rows(kt):
            return pl.ds(pl.multiple_of(kt * T, T), T)

        wis = wiT_ref[...] * idx_scale

        def fold8(a):
            return jnp.sum(a.reshape(a.shape[0] // 8, 8, T), axis=0)

        HT = T // 2

        raw_ref = s_ref.at[0]

        def p1_half(kt_dot, kt_post, half, carry):
            hs = slice(half * HT, (half + 1) * HT)
            if kt_post is not None:
                mn, mx, cgt0, cge0 = carry
                acc = raw_ref[hs, :]
                mn = jnp.minimum(mn, jnp.min(acc, axis=0, keepdims=True))
                mx = jnp.maximum(mx, jnp.max(acc, axis=0, keepdims=True))
                slab = (kt_post == j).astype(jnp.int32) + 2 * (kt_post > j).astype(jnp.int32)
                sc = acc + adm_ref[slab, hs, :]
                sc_ref[pl.ds(pl.multiple_of(kt_post * T + half * HT, HT), HT), :] = sc
                cgt0 = cgt0 + fold8(jnp.where(sc > 0.0, 1.0, 0.0))
                cge0 = cge0 + fold8(jnp.where(sc >= 0.0, 1.0, 0.0))
                carry = (mn, mx, cgt0, cge0)
            if kt_dot is not None:
                kt_ld = jnp.minimum(kt_dot, pl.num_programs(1) - 1)
                ki_t = ki_ref[pl.ds(pl.multiple_of(kt_ld * T + half * HT, HT), HT), :]
                acc = jnp.zeros((HT, T), F32)
                for h in range(HEADS):
                    lg = jnp.dot(ki_t, qiT_ref[HEAD_DIM * h:HEAD_DIM * (h + 1), :],
                                 preferred_element_type=F32)
                    acc = acc + jnp.maximum(lg, 0.0) * wis[h:h + 1, :]
                raw_ref[hs, :] = acc
            return carry

        ntp = nt + nt % 2

        def p1_tile(kt_dot, kt_post, carry):
            for half in range(2):
                carry = p1_half(kt_dot, kt_post, half, carry)
            return carry

        def p1(i, carry):
            carry = p1_tile(2 * i + 1, 2 * i, carry)
            return p1_tile(2 * i + 2, 2 * i + 1, carry)

        z8 = jnp.zeros((8, T), F32)
        carry = (jnp.full((1, T), jnp.inf, F32), jnp.full((1, T), -jnp.inf, F32), z8, z8)
        p1_tile(0, None, None)
        carry = lax.fori_loop(0, ntp // 2 - 1, p1, carry)
        carry = p1_tile(ntp - 1, ntp - 2, carry)
        carry = p1_tile(None, ntp - 1, carry)
        lo0, hi0, cgt0, cge0 = carry
        cgt0 = jnp.sum(cgt0, axis=0, keepdims=True)
        cge0 = jnp.sum(cge0, axis=0, keepdims=True)

        qpos = j * T + lax.broadcasted_iota(jnp.int32, (1, T), 1)
        n_adm = ((qpos // CHUNK + 1) * CHUNK).astype(F32)
        small = n_adm <= kf
        above0 = cgt0 >= kf
        below0 = cge0 < kf
        zero_tie = jnp.logical_not(jnp.logical_or(above0, below0))
        lo_pos = lo0 > 0.0
        lo1 = jnp.where(zero_tie, 0.0, jnp.where(jnp.logical_and(above0, jnp.logical_not(lo_pos)),
                                                 0.0, lo0))
        cl1 = jnp.where(jnp.logical_or(zero_tie, jnp.logical_and(above0, jnp.logical_not(lo_pos))),
                        cge0, n_adm)
        hi1 = jnp.where(zero_tie, 0.0, jnp.where(below0, jnp.minimum(hi0, 0.0), hi0))

        def count_ge(th):
            def body(i, c8):
                for u in range(2):
                    blk = sc_ref[tile_rows(2 * i + u), :]
                    c8 = c8 + fold8(jnp.where(blk >= th, 1.0, 0.0))
                return c8
            return jnp.sum(lax.fori_loop(0, ntp // 2, body, z8), axis=0, keepdims=True)

        def any_open(done):
            return jnp.max(jnp.where(done, 0.0, 1.0)) > 0.0

        def settled(cl_):
            return jnp.logical_or(jnp.logical_or(small, zero_tie), cl_ == kf)

        def bisect(lo, hi, cl):
            mid = 0.5 * lo + 0.5 * hi
            c = count_ge(mid)
            ge = c >= kf
            return jnp.where(ge, mid, lo), jnp.where(ge, hi, mid), jnp.where(ge, c, cl)

        def b_cond(st):
            it, _, _, _, open_ = st
            return jnp.logical_and(it < BISECT_MAX_ITERS, open_)

        def b_body(st):
            it, lo, hi, cl, _ = st
            lo, hi, cl = bisect(lo, hi, cl)
            return it + 1, lo, hi, cl, any_open(settled(cl))

        open1 = any_open(settled(cl1))
        n_blind = jnp.where(open1, BISECT_BLIND_ITERS, 0)
        lo, hi, cl = lax.fori_loop(0, n_blind, lambda _, st: bisect(*st), (lo1, hi1, cl1))
        _, lo, hi, cl, open_ = lax.while_loop(
            b_cond, b_body, (n_blind, lo, hi, cl, jnp.logical_and(open1, any_open(settled(cl)))))
        has_ties = jnp.logical_or(open_, jnp.max(jnp.where(
            jnp.logical_and(zero_tie, jnp.logical_not(small)), 1.0, 0.0)) > 0.0)

        def fast_mask():
            thr = jnp.where(small, F32_MIN, lo)

            def body(kt, _):
                blk = sc_ref[tile_rows(kt), :]
                sc_ref[tile_rows(kt), :] = jnp.where(blk >= thr, 0.0, NEG)
                return 0
            lax.fori_loop(0, nt, body, 0)

        def exact_mask():
            def done_of(lo_, hi_, cl_):
                return jnp.logical_or(jnp.logical_or(small, cl_ == kf), lo_ >= hi_)

            def s_body(st):
                lo_, hi_, cl_, _ = st
                done = done_of(lo_, hi_, cl_)
                mid = 0.5 * lo_ + 0.5 * hi_
                mid = jnp.where(mid > lo_, mid, hi_)

                def body(kt, c3):
                    c, mn_ge, mx_lt = c3
                    blk = sc_ref[tile_rows(kt), :]
                    ge_ = blk >= mid
                    c = c + jnp.sum(jnp.where(ge_, 1.0, 0.0), axis=0, keepdims=True)
                    mn_ge = jnp.minimum(mn_ge, jnp.min(jnp.where(ge_, blk, jnp.inf),
                                                       axis=0, keepdims=True))
                    mx_lt = jnp.maximum(mx_lt, jnp.max(jnp.where(ge_, -jnp.inf, blk),
                                                       axis=0, keepdims=True))
                    return c, mn_ge, mx_lt
                c, mn_ge, mx_lt = lax.fori_loop(
                    0, nt, body, (zrow, jnp.full((1, T), jnp.inf, F32),
                                  jnp.full((1, T), -jnp.inf, F32)))
                ge = c >= kf
                lo_n = jnp.where(done, lo_, jnp.where(ge, mn_ge, lo_))
                cl_n = jnp.where(done, cl_, jnp.where(ge, c, cl_))
                hi_n = jnp.where(done, hi_, jnp.where(ge, hi_, mx_lt))
                return lo_n, hi_n, cl_n, any_open(done_of(lo_n, hi_n, cl_n))

            lo_e, _, cl_e, _ = lax.while_loop(
                lambda st: st[3], s_body, (lo, hi, cl, any_open(done_of(lo, hi, cl))))
            thr = jnp.where(small, F32_MIN, lo_e)

            def count_gt():
                def body(kt, c8):
                    blk = sc_ref[tile_rows(kt), :]
                    return c8 + fold8(jnp.where(blk > thr, 1.0, 0.0))
                return jnp.sum(lax.fori_loop(0, nt, body, z8), axis=0, keepdims=True)
            cgt = lax.cond(open_, count_gt, lambda: cgt0)
            need = jnp.where(jnp.logical_or(small, cl_e == kf), jnp.inf, kf - cgt)

            UNDECIDED = -1.0

            def m_body(kt, st):
                carry, xt, xc = st
                blk = sc_ref[tile_rows(kt), :]
                eq = jnp.where(blk == thr, 1.0, 0.0)
                cnt = jnp.sum(fold8(eq), axis=0, keepdims=True)
                has_budget = carry < need
                crossing = jnp.logical_and(has_budget, carry + cnt > need)
                tie_val = jnp.where(crossing, UNDECIDED, jnp.where(has_budget, 0.0, NEG))
                sc_ref[tile_rows(kt), :] = jnp.where(
                    blk > thr, 0.0, jnp.where(blk == thr, tie_val, NEG))
                xt = jnp.where(crossing, kt.astype(F32), xt)
                xc = jnp.where(crossing, carry, xc)
                return carry + cnt, xt, xc
            _, xt, xc = lax.fori_loop(0, nt, m_body, (zrow, jnp.full((1, T), -1.0, F32), zrow))

            def r_body(st):
                xt_, ktf = st
                kt = ktf.astype(jnp.int32)
                msk = sc_ref[tile_rows(kt), :]
                und = msk == UNDECIDED
                r_io = lax.broadcasted_iota(jnp.int32, (T, T), 0)
                c_io = lax.broadcasted_iota(jnp.int32, (T, T), 1)
                ltri = jnp.where(c_io < r_io, 1.0, 0.0).astype(BF16)
                rank = jnp.dot(ltri, jnp.where(und, 1.0, 0.0).astype(BF16),
                               preferred_element_type=F32) + xc
                sc_ref[tile_rows(kt), :] = jnp.where(
                    und, jnp.where(rank < need, 0.0, NEG), msk)
                xt_ = jnp.where(xt_ == ktf, -1.0, xt_)
                return xt_, jnp.max(xt_)
            lax.while_loop(lambda st: st[1] >= 0.0, r_body, (xt, jnp.max(xt)))

        lax.cond(has_ties, exact_mask, fast_mask)

        m_ref[...] = jnp.full(m_ref.shape, NEG, F32)
        acc_ref[...] = jnp.zeros(acc_ref.shape, F32)
        vx_ref[HEAD_DIM:, :] = jnp.ones((VX_ROWS - HEAD_DIM, T), BF16)

        def step(kt_a, bias_slab, kt_bc):
            if kt_bc is not None:
                vx_ref[0:HEAD_DIM, :] = vbT_ref[kt_bc]
                m_old = m_ref[...]
                m_new = jnp.maximum(m_old, tmax_ref[...])
                alpha = jnp.exp2(m_old - m_new)
                m_ref[...] = m_new
            if kt_a is not None:
                kb_t = kb_ref[tile_rows(kt_a), :]
                msk = sc_ref[tile_rows(kt_a), :]
            tmax = []
            for h in range(HEADS):
                rows = slice(HEAD_DIM * h, HEAD_DIM * (h + 1))
                if kt_bc is not None:
                    p_ref[h] = jnp.exp2(s_ref[h] - m_new[h:h + 1, :]).astype(BF16)
                if kt_a is not None:
                    add = msk if bias_slab is None else msk + gbias_ref[bias_slab, h]
                    s = jnp.dot(kb_t, qbT_ref[rows, :], preferred_element_type=F32) + add
                    s_ref[h] = s
                    tmax.append(jnp.max(s, axis=0, keepdims=True))
                if kt_bc is not None:
                    for hc in ([h - PV_LAG] if h < HEADS - 1 else range(h - PV_LAG, HEADS)):
                        if hc < 0:
                            continue
                        acc_ref[hc] = alpha[hc:hc + 1, :] * acc_ref[hc] + jnp.dot(
                            vx_ref[...], p_ref[hc], preferred_element_type=F32)
            if kt_a is not None:
                tmax_ref[...] = jnp.concatenate(tmax, axis=0)

        n_far = jnp.maximum(j - 1, 0)
        step(0, jnp.clip(2 - j, 0, 2), None)

        def far_step(a, _):
            step(a, None, a - 1)
            return 0

        def near_step(a, _):
            step(a, a - j + 2, a - 1)
            return 0

        lax.fori_loop(1, n_far, far_step, 0)
        lax.fori_loop(jnp.maximum(n_far, 1), nt, near_step, 0)
        step(None, None, nt - 1)

        outs = []
        for h in range(HEADS):
            a = acc_ref[h]
            outs.append(a[0:HEAD_DIM, :] * (1.0 / a[HEAD_DIM:HEAD_DIM + 1, :]))
        o_ref[...] = jnp.concatenate(outs, axis=0).T

    return kernel


def _dsa(qbT, qiT, wiT, ki3, kb3, vbT4, gbias, adm, n_sel):
    b, s, _ = ki3.shape
    T = DSA_T
    nq = s // T
    grid = (b, nq)
    in_specs = [
        pl.BlockSpec((WIDTH, T), lambda bi, j: (0, bi * nq + j)),
        pl.BlockSpec((WIDTH, T), lambda bi, j: (0, bi * nq + j)),
        pl.BlockSpec((HEADS, T), lambda bi, j: (0, bi * nq + j)),
        pl.BlockSpec((None, s, HEAD_DIM), lambda bi, j: (bi, 0, 0)),
        pl.BlockSpec((None, s, HEAD_DIM), lambda bi, j: (bi, 0, 0)),
        pl.BlockSpec((None, nq, HEAD_DIM, T), lambda bi, j: (bi, 0, 0, 0)),
        pl.BlockSpec(gbias.shape, lambda bi, j: (0, 0, 0, 0)),
        pl.BlockSpec(adm.shape, lambda bi, j: (0, 0, 0)),
    ]
    out_specs = pl.BlockSpec((None, T, WIDTH), lambda bi, j: (bi, j, 0))
    scratch = [
        pltpu.VMEM((s + T, T), F32),
        pltpu.VMEM((HEADS, T), F32),
        pltpu.VMEM((HEADS, VX_ROWS, T), F32),
        pltpu.VMEM((HEADS, T), F32),
        pltpu.VMEM((HEADS, T, T), F32),
        pltpu.VMEM((HEADS, T, T), BF16),
        pltpu.VMEM((VX_ROWS, T), BF16),
    ]
    blk = (2 * (2 * _nbytes((WIDTH, T), BF16) + _nbytes((HEADS, T), F32)
                + 2 * _nbytes((s, V7X_LANES), BF16) + _nbytes((HEAD_DIM, s), BF16)
                + _nbytes(gbias.shape, F32) + _nbytes(adm.shape, F32)
                + _nbytes((T, WIDTH), F32))
           + _nbytes((s + T, T), F32) + 2 * _nbytes((HEADS, T), F32)
           + _nbytes((HEADS, VX_ROWS, T), F32) + _nbytes((HEADS, T, T), F32)
           + _nbytes((HEADS, T, T), BF16) + _nbytes((VX_ROWS, T), BF16))
    return pl.pallas_call(
        _make_dsa_kernel(n_sel), out_shape=jax.ShapeDtypeStruct((b, s, WIDTH), F32),
        grid=grid, in_specs=in_specs, out_specs=out_specs, scratch_shapes=scratch, name="dsa",
        compiler_params=pltpu.CompilerParams(
            dimension_semantics=("arbitrary", "arbitrary"), vmem_limit_bytes=_vmem_limit(blk)),
    )(qbT, qiT, wiT, ki3, kb3, vbT4, gbias, adm)


def _merge_kernel(x_ref, ya_ref, yb_ref, gate_ref, wa_ref, wb_ref, wo_ref, fg_ref, o_ref):
    d = x_ref.shape[1]
    za = gate_ref[:, 0:WIDTH].astype(F32)
    zb = gate_ref[:, WIDTH:2 * WIDTH].astype(F32)
    ga = gate_ref[:, 2 * WIDTH:2 * WIDTH + d].astype(F32)
    gb = gate_ref[:, 2 * WIDTH + d:2 * WIDTH + 2 * d].astype(F32)
    ua = (ya_ref[...] * (za * _sigmoid(za))).astype(BF16)
    ub = (yb_ref[...] * (zb * _sigmoid(zb))).astype(BF16)
    pa = jnp.dot(ua, wa_ref[...], preferred_element_type=F32)
    pb = jnp.dot(ub, wb_ref[...], preferred_element_type=F32)
    merged = _sigmoid(ga) * pa + _sigmoid(gb) * pb
    h = x_ref[...] + jnp.dot(merged.astype(BF16), wo_ref[...], preferred_element_type=F32)
    ms = jnp.mean(h * h, axis=-1, keepdims=True)
    o_ref[...] = (h * lax.rsqrt(ms + EPS)) * fg_ref[...]


def _merge(x2, ya2, yb2, gates, wa, wb, wo, fg, tm):
    n, d = x2.shape
    n_gate = gates.shape[1]
    grid = (n // tm,)
    row = lambda i: (i, 0)
    const = lambda i: (0, 0)
    in_specs = [
        pl.BlockSpec((tm, d), row),
        pl.BlockSpec((tm, WIDTH), row),
        pl.BlockSpec((tm, WIDTH), row),
        pl.BlockSpec((tm, n_gate), row),
        pl.BlockSpec(wa.shape, const),
        pl.BlockSpec(wb.shape, const),
        pl.BlockSpec(wo.shape, const),
        pl.BlockSpec((1, d), const),
    ]
    blk = 2 * (2 * _nbytes((tm, d), F32) + 2 * _nbytes((tm, WIDTH), F32)
               + _nbytes((tm, n_gate), BF16) + _nbytes(wa.shape, BF16) + _nbytes(wb.shape, BF16)
               + _nbytes(wo.shape, BF16))
    return pl.pallas_call(
        _merge_kernel, out_shape=jax.ShapeDtypeStruct((n, d), F32),
        grid=grid, in_specs=in_specs, out_specs=pl.BlockSpec((tm, d), row), name="merge",
        compiler_params=pltpu.CompilerParams(
            dimension_semantics=("arbitrary",), vmem_limit_bytes=_vmem_limit(blk)),
    )(x2, ya2, yb2, gates, wa, wb, wo, fg)


def _toeplitz(g, nrows, ncols):
    period = nrows + ncols
    lead = g.shape[:-1]
    p = jnp.concatenate([g[..., ::-1], jnp.zeros(lead + (1,), g.dtype)], axis=-1)
    flat = jnp.tile(p, (1,) * len(lead) + (nrows,))[..., :nrows * (period - 1)]
    x = flat.reshape(lead + (nrows, period - 1))
    return x[..., nrows - 1:nrows - 1 + ncols]


def _band_bias(a_rel_bias):
    pad = A_LEFT_CHUNKS * CHUNK
    rel = np.arange(BAND_TQ + BAND_WIN - 1) - (BAND_WIN - 1) + pad
    idx = np.clip(rel, -A_REL_CLIP, A_REL_CLIP) + A_REL_CLIP
    lo, hi = int(idx[0]), int(idx.max())
    n_flat = int((idx == hi).sum()) - 1
    assert np.array_equal(idx, np.minimum(np.arange(lo, lo + idx.size), hi))
    ab = a_rel_bias.astype(F32)
    g = jnp.concatenate([ab[:, lo:hi + 1], jnp.broadcast_to(ab[:, hi:hi + 1], (HEADS, n_flat))],
                        axis=1)
    bias = _toeplitz(g, BAND_TQ, BAND_WIN)
    qq = np.arange(BAND_TQ)[:, None]
    jj = np.arange(BAND_WIN)[None, :]
    in_band = np.logical_and(jj // CHUNK >= qq // CHUNK, jj // CHUNK <= qq // CHUNK + A_LEFT_CHUNKS)
    bias = jnp.where(jnp.asarray(in_band)[None], bias * LOG2E, NEG)
    return jnp.swapaxes(bias, 1, 2)


def _dsa_bias(t5_bias):
    T = DSA_T
    far = T5_BUCKETS // 2 - 1
    tb = t5_bias.astype(F32)
    slabs = [jnp.zeros((HEADS, T, T), F32)]
    for off in (-T, 0):
        rel = jnp.arange(2 * T - 1, dtype=jnp.int32) - (T - 1) + off
        g = (tb[_t5_bucket(rel)] - tb[far]) * LOG2E
        slabs.append(_toeplitz(g.T, T, T))
    return jnp.stack(slabs)


def _far_bucket_is_constant(s):
    half = T5_BUCKETS // 2
    max_exact = half // 2
    for n in (DSA_T + 1, max(s - 1, DSA_T + 1)):
        large = max_exact + int(math.log(n / max_exact) / math.log(T5_MAX_DIST / max_exact)
                                * (half - max_exact))
        if min(large, half - 1) != half - 1:
            return False
    return True


def kernel(x, norm_gain, w_in, a_rel_bias, t5_bias, w_a_out, w_b_out, w_out, final_gain):
    b, s, d = x.shape
    depth = w_in.shape[0]
    n = b * s
    assert s % DSA_T == 0 and s % BAND_TQ == 0
    assert _far_bucket_is_constant(s)
    tm = 512 if n % 512 == 0 else DSA_T
    n_sel = min(TOPK_MAX, s // 4)

    splits = (WIDTH,) * 4 + (WIDTH, HEAD_DIM, HEAD_DIM, WIDTH) + (WIDTH, HEAD_DIM, HEADS) + (d, d)
    offs = np.concatenate([[0], np.cumsum(splits)])
    assert offs[-1] == w_in.shape[2]

    abias = _band_bias
    gbias = _dsa_bias(t5_bias)
    kk = jnp.arange(DSA_T)[:, None] // CHUNK
    qq = jnp.arange(DSA_T)[None, :] // CHUNK
    adm = jnp.stack([jnp.zeros((DSA_T, DSA_T), F32),
                     jnp.where(kk <= qq, 0.0, -jnp.inf).astype(F32),
                     jnp.full((DSA_T, DSA_T), -jnp.inf, F32)])

    h2 = x.reshape(n, d)
    for l in range(depth):
        w = w_in[l]
        (qa, ka, va, za, qb, kb, vb, zb, qi, ki, wi, ga, gb) = [
            w[:, offs[i]:offs[i + 1]] for i in range(len(splits))]
        wn = jnp.concatenate([ka, ki, kb], axis=1).astype(BF16)
        wg = jnp.concatenate([za, zb, ga, gb], axis=1).astype(BF16)
        qscale = HEAD_DIM ** -0.5 * LOG2E
        wt = jnp.concatenate([qa * qscale, va, qb * qscale, qi, vb, wi,
                              jnp.zeros((d, HEADS), w.dtype)], axis=1).T.astype(BF16)

        ka_o, ki_o, kb_o, gates, qaT, vaT, qbT, qiT, vbT, wiT = _proj(
            h2, norm_gain[l].reshape(1, d).astype(F32), wn, wg, wt, tm)

        ya = _band(qaT, ka_o.reshape(b, s, WIDTH),
                   vaT.reshape(b, s // BAND_KB, WIDTH, BAND_KB), abias(a_rel_bias[l]))
        yb = _dsa(qbT, qiT, wiT, ki_o.reshape(b, s, HEAD_DIM), kb_o.reshape(b, s, HEAD_DIM),
                  vbT.reshape(b, s // DSA_T, HEAD_DIM, DSA_T), gbias, adm, n_sel)

        assert depth == 1
        h2 = _merge(h2, ya.reshape(n, WIDTH), yb.reshape(n, WIDTH), gates,
                    w_a_out[l].astype(BF16), w_b_out[l].astype(BF16), w_out[l].astype(BF16),
                    final_gain.reshape(1, d).astype(F32), tm)
    return h2.reshape(b, s, d)
```

```python
import math

import numpy as np
import jax
import jax.numpy as jnp
from jax import lax
from jax.experimental import pallas as pl
from jax.experimental.pallas import tpu as pltpu

F32 = jnp.float32
BF16 = jnp.bfloat16

CHUNK = 64
EPS = 1e-6
HEADS = 8
HEAD_DIM = 64
WIDTH = HEADS * HEAD_DIM
A_LEFT_CHUNKS = 8
A_REL_CLIP = 256
TOPK_MAX = 256
T5_BUCKETS = 32
T5_MAX_DIST = 128
NEG = -1e30
F32_MIN = float(np.finfo(np.float32).min)
LOG2E = math.log2(math.e)

V7X_LANES = 128
V7X_SUBLANES = 8
VMEM_CAP_BYTES = 60000 * 1024
COMPILER_TEMP_BYTES = 8 * 1024 * 1024

BAND_TQ = 4 * CHUNK
BAND_WIN = A_LEFT_CHUNKS * CHUNK + BAND_TQ
BAND_KB = V7X_LANES
BAND_NBLK = BAND_WIN // BAND_KB
DSA_T = 256
VX_ROWS = HEAD_DIM + 2 * V7X_SUBLANES
PV_LAG = 2
BISECT_MAX_ITERS = 26
BISECT_BLIND_ITERS = 17


def _vmem_limit(block_bytes):
    return int(min(block_bytes + COMPILER_TEMP_BYTES, VMEM_CAP_BYTES))


def _nbytes(shape, dtype):
    return int(np.prod(shape)) * jnp.dtype(dtype).itemsize


def _t5_bucket(rel):
    half = T5_BUCKETS // 2
    max_exact = half // 2
    ret = jnp.where(rel > 0, half, 0)
    n = jnp.abs(rel)
    nf = jnp.maximum(n, 1).astype(F32)
    large = max_exact + (jnp.log(nf / max_exact) / math.log(T5_MAX_DIST / max_exact)
                         * (half - max_exact)).astype(jnp.int32)
    large = jnp.minimum(large, half - 1)
    return ret + jnp.where(n < max_exact, n, large)


def _sigmoid(v):
    return 1.0 / (1.0 + jnp.exp(-v))


def _proj_kernel(x_ref, g_ref, wn_ref, wg_ref, wt_ref,
                 ka_ref, ki_ref, kb_ref, gate_ref,
                 qaT_ref, vaT_ref, qbT_ref, qiT_ref, vbT_ref, wiT_ref):
    xf = x_ref[...]
    ms = jnp.mean(xf * xf, axis=-1, keepdims=True)
    hn = ((xf * lax.rsqrt(ms + EPS)) * g_ref[...]).astype(BF16)
    tm = hn.shape[0]

    ka_ref[...] = jnp.dot(hn, wn_ref[:, 0:WIDTH], preferred_element_type=F32).astype(BF16)
    kk = jnp.dot(hn, wn_ref[:, WIDTH:WIDTH + 2 * HEAD_DIM], preferred_element_type=F32)
    ki_ref[...] = kk[:, :HEAD_DIM].astype(BF16)
    kb_ref[...] = kk[:, HEAD_DIM:].astype(BF16)
    n_gate = gate_ref.shape[1]
    for c0 in range(0, n_gate, WIDTH):
        gate_ref[:, c0:c0 + WIDTH] = jnp.dot(
            hn, wg_ref[:, c0:c0 + WIDTH], preferred_element_type=F32).astype(BF16)

    def t_rows(r0, nrows):
        return lax.dot_general(wt_ref[r0:r0 + nrows, :], hn, (((1,), (1,)), ((), ())),
                               preferred_element_type=F32)
    qaT_ref[...] = t_rows(0, WIDTH).astype(BF16)
    vaT = t_rows(WIDTH, WIDTH).astype(BF16)
    for s in range(tm // BAND_KB):
        vaT_ref[s] = vaT[:, s * BAND_KB:(s + 1) * BAND_KB]
    qbT_ref[...] = t_rows(2 * WIDTH, WIDTH).astype(BF16)
    qiT_ref[...] = t_rows(3 * WIDTH, WIDTH).astype(BF16)
    tail = t_rows(4 * WIDTH, HEAD_DIM + 2 * HEADS)
    for s in range(tm // DSA_T):
        vbT_ref[s] = tail[0:HEAD_DIM, s * DSA_T:(s + 1) * DSA_T].astype(BF16)
    wiT_ref[...] = tail[HEAD_DIM:HEAD_DIM + HEADS, :]


def _proj(x2, gain, wn, wg, wt, tm):
    n, d = x2.shape
    n_gate = wg.shape[1]
    grid = (n // tm,)
    const = lambda i: (0, 0)
    row = lambda i: (i, 0)
    col = lambda i: (0, i)
    in_specs = [
        pl.BlockSpec((tm, d), row),
        pl.BlockSpec((1, d), const),
        pl.BlockSpec(wn.shape, const),
        pl.BlockSpec(wg.shape, const),
        pl.BlockSpec(wt.shape, const),
    ]
    out_shape = (
        jax.ShapeDtypeStruct((n, WIDTH), BF16),
        jax.ShapeDtypeStruct((n, HEAD_DIM), BF16),
        jax.ShapeDtypeStruct((n, HEAD_DIM), BF16),
        jax.ShapeDtypeStruct((n, n_gate), BF16),
        jax.ShapeDtypeStruct((WIDTH, n), BF16),
        jax.ShapeDtypeStruct((n // BAND_KB, WIDTH, BAND_KB), BF16),
        jax.ShapeDtypeStruct((WIDTH, n), BF16),
        jax.ShapeDtypeStruct((WIDTH, n), BF16),
        jax.ShapeDtypeStruct((n // DSA_T, HEAD_DIM, DSA_T), BF16),
        jax.ShapeDtypeStruct((HEADS, n), F32),
    )
    out_specs = (
        pl.BlockSpec((tm, WIDTH), row),
        pl.BlockSpec((tm, HEAD_DIM), row),
        pl.BlockSpec((tm, HEAD_DIM), row),
        pl.BlockSpec((tm, n_gate), row),
        pl.BlockSpec((WIDTH, tm), col),
        pl.BlockSpec((tm // BAND_KB, WIDTH, BAND_KB), lambda i: (i, 0, 0)),
        pl.BlockSpec((WIDTH, tm), col),
        pl.BlockSpec((WIDTH, tm), col),
        pl.BlockSpec((tm // DSA_T, HEAD_DIM, DSA_T), lambda i: (i, 0, 0)),
        pl.BlockSpec((HEADS, tm), col),
    )
    blk = 2 * (_nbytes((tm, d), F32) + _nbytes(wn.shape, BF16) + _nbytes(wg.shape, BF16)
               + _nbytes(wt.shape, BF16) + _nbytes((tm, WIDTH), BF16)
               + 2 * _nbytes((tm, V7X_LANES), BF16) + _nbytes((tm, n_gate), BF16)
               + 4 * _nbytes((WIDTH, tm), BF16) + _nbytes((HEAD_DIM, tm), BF16)
               + _nbytes((HEADS, tm), F32))
    return pl.pallas_call(
        _proj_kernel, out_shape=out_shape, grid=grid, in_specs=in_specs, out_specs=out_specs,
        name="proj",
        compiler_params=pltpu.CompilerParams(
            dimension_semantics=("arbitrary",), vmem_limit_bytes=_vmem_limit(blk)),
    )(x2, gain, wn, wg, wt)


def _band_kernel(qT_ref, k_ref, vT_ref, bias_ref, o_ref, s_ref, p_ref):
    t = pl.program_id(1)
    row = lax.broadcasted_iota(jnp.int32, (V7X_LANES, BAND_TQ), 0)
    first_half = row < HEAD_DIM
    n_left = A_LEFT_CHUNKS * CHUNK // BAND_KB
    blocks = []
    for i in range(BAND_NBLK):
        kidx = t * (BAND_TQ // BAND_KB) - n_left + i
        kc = jnp.maximum(kidx, 0)
        blocks.append((kidx, kc, pl.multiple_of(kc * BAND_KB, BAND_KB)))

    def stage_a(h):
        cs = slice(V7X_LANES * (h // 2), V7X_LANES * (h // 2 + 1))
        qT = qT_ref[cs, :]
        qe = jnp.where(first_half if h % 2 == 0 else jnp.logical_not(first_half), qT,
                       jnp.zeros_like(qT))
        m_el = None
        for i, (kidx, kc, ks) in enumerate(blocks):
            ksl = slice(BAND_KB * i, BAND_KB * (i + 1))
            s = jnp.dot(k_ref[pl.ds(ks, BAND_KB), cs], qe, preferred_element_type=F32)
            s = s + bias_ref[h, ksl, :]
            if i < n_left:
                s = jnp.where(kidx >= 0, s, NEG)
            s_ref[h, ksl, :] = s
            m_el = s if m_el is None else jnp.maximum(m_el, s)
        return jnp.max(m_el, axis=0, keepdims=True)

    ones = jnp.ones((VX_ROWS - HEAD_DIM, BAND_KB), BF16)

    def stage_bc(h, m_h):
        p_ref[h] = jnp.exp2(s_ref[h] - m_h).astype(BF16)
        acc = None
        for i, (kidx, kc, ks) in enumerate(blocks):
            vx = jnp.concatenate([vT_ref[kc, HEAD_DIM * h:HEAD_DIM * (h + 1), :], ones], axis=0)
            pv = jnp.dot(vx, p_ref[h, BAND_KB * i:BAND_KB * (i + 1), :],
                         preferred_element_type=F32)
            acc = pv if acc is None else acc + pv
        return acc[0:HEAD_DIM, :] * (1.0 / acc[HEAD_DIM:HEAD_DIM + 1, :])

    m = [stage_a(h) for h in range(HEADS)]
    outs = [stage_bc(h, m[h]) for h in range(HEADS)]
    o_ref[...] = jnp.concatenate(outs, axis=0).T


def _band(qaT, ka3, vaT4, abiasT):
    b, s, _ = ka3.shape
    nt = s // BAND_TQ
    grid = (b, nt)
    once = pl.Buffered(1)
    in_specs = [
        pl.BlockSpec((WIDTH, BAND_TQ), lambda bi, t: (0, bi * nt + t)),
        pl.BlockSpec((None, s, WIDTH), lambda bi, t: (bi, 0, 0), pipeline_mode=once),
        pl.BlockSpec((None, s // BAND_KB, WIDTH, BAND_KB), lambda bi, t: (bi, 0, 0, 0),
                     pipeline_mode=once),
        pl.BlockSpec(abiasT.shape, lambda bi, t: (0, 0, 0), pipeline_mode=once),
    ]
    out_specs = pl.BlockSpec((None, BAND_TQ, WIDTH), lambda bi, t: (bi, t, 0))
    scratch = [
        pltpu.VMEM((HEADS, BAND_WIN, BAND_TQ), F32),
        pltpu.VMEM((HEADS, BAND_WIN, BAND_TQ), BF16),
    ]
    blk = (2 * (_nbytes((WIDTH, BAND_TQ), BF16) + _nbytes((BAND_TQ, WIDTH), F32))
           + 2 * _nbytes((s, WIDTH), BF16) + _nbytes(abiasT.shape, F32)
           + _nbytes((HEADS, BAND_WIN, BAND_TQ), F32) + _nbytes((HEADS, BAND_WIN, BAND_TQ), BF16))
    return pl.pallas_call(
        _band_kernel, out_shape=jax.ShapeDtypeStruct((b, s, WIDTH), F32),
        grid=grid, in_specs=in_specs, out_specs=out_specs, scratch_shapes=scratch, name="band",
        compiler_params=pltpu.CompilerParams(
            dimension_semantics=("arbitrary", "arbitrary"), vmem_limit_bytes=_vmem_limit(blk)),
    )(qaT, ka3, vaT4, abiasT)


def _make_dsa_kernel(n_sel):
    T = DSA_T
    kf = float(n_sel)
    idx_scale = (HEADS ** -0.5) * (HEAD_DIM ** -0.5)

    def kernel(qbT_ref, qiT_ref, wiT_ref, ki_ref, kb_ref, vbT_ref, gbias_ref, adm_ref, o_ref,
               sc_ref, m_ref, acc_ref, tmax_ref, s_ref, p_ref, vx_ref):
        j = pl.program_id(1)
        nt = j + 1
        zrow = jnp.zeros((1, T), F32)

        def tile_rows(kt):
            return pl.ds(pl.multiple_of(kt * T, T), T)

        wis = wiT_ref[...] * idx_scale

        def fold8(a):
            return jnp.sum(a.reshape(a.shape[0] // V7X_SUBLANES, V7X_SUBLANES, T), axis=0)

        HT = T // 2

        raw_ref = s_ref.at[0]

        def p1_half(kt_dot, kt_post, half, carry):
            hs = slice(half * HT, (half + 1) * HT)
            if kt_post is not None:
                mn, mx, cgt0, cge0 = carry
                acc = raw_ref[hs, :]
                mn = jnp.minimum(mn, jnp.min(acc, axis=0, keepdims=True))
                mx = jnp.maximum(mx, jnp.max(acc, axis=0, keepdims=True))
                slab = (kt_post == j).astype(jnp.int32) + 2 * (kt_post > j).astype(jnp.int32)
                sc = acc + adm_ref[slab, hs, :]
                sc_ref[pl.ds(pl.multiple_of(kt_post * T + half * HT, HT), HT), :] = sc
                cgt0 = cgt0 + fold8(jnp.where(sc > 0.0, 1.0, 0.0))
                cge0 = cge0 + fold8(jnp.where(sc >= 0.0, 1.0, 0.0))
                carry = (mn, mx, cgt0, cge0)
            if kt_dot is not None:
                kt_ld = jnp.minimum(kt_dot, pl.num_programs(1) - 1)
                ki_t = ki_ref[pl.ds(pl.multiple_of(kt_ld * T + half * HT, HT), HT), :]
                acc = jnp.zeros((HT, T), F32)
                for h in range(HEADS):
                    lg = jnp.dot(ki_t, qiT_ref[HEAD_DIM * h:HEAD_DIM * (h + 1), :],
                                 preferred_element_type=F32)
                    acc = acc + jnp.maximum(lg, 0.0) * wis[h:h + 1, :]
                raw_ref[hs, :] = acc
            return carry

        ntp = nt + nt % 2

        def p1_tile(kt_dot, kt_post, carry):
            for half in range(2):
                carry = p1_half(kt_dot, kt_post, half, carry)
            return carry

        def p1(i, carry):
            carry = p1_tile(2 * i + 1, 2 * i, carry)
            return p1_tile(2 * i + 2, 2 * i + 1, carry)

        z8 = jnp.zeros((V7X_SUBLANES, T), F32)
        carry = (jnp.full((1, T), jnp.inf, F32), jnp.full((1, T), -jnp.inf, F32), z8, z8)
        p1_tile(0, None, None)
        carry = lax.fori_loop(0, ntp // 2 - 1, p1, carry)
        carry = p1_tile(ntp - 1, ntp - 2, carry)
        carry = p1_tile(None, ntp - 1, carry)
        lo0, hi0, cgt0, cge0 = carry
        cgt0 = jnp.sum(cgt0, axis=0, keepdims=True)
        cge0 = jnp.sum(cge0, axis=0, keepdims=True)

        qpos = j * T + lax.broadcasted_iota(jnp.int32, (1, T), 1)
        n_adm = ((qpos // CHUNK + 1) * CHUNK).astype(F32)
        small = n_adm <= kf
        above0 = cgt0 >= kf
        below0 = cge0 < kf
        zero_tie = jnp.logical_not(jnp.logical_or(above0, below0))
        lo_pos = lo0 > 0.0
        lo1 = jnp.where(zero_tie, 0.0, jnp.where(jnp.logical_and(above0, jnp.logical_not(lo_pos)),
                                                 0.0, lo0))
        cl1 = jnp.where(jnp.logical_or(zero_tie, jnp.logical_and(above0, jnp.logical_not(lo_pos))),
                        cge0, n_adm)
        hi1 = jnp.where(zero_tie, 0.0, jnp.where(below0, jnp.minimum(hi0, 0.0), hi0))

        def count_ge(th):
            def body(i, c8):
                for u in range(2):
                    blk = sc_ref[tile_rows(2 * i + u), :]
                    c8 = c8 + fold8(jnp.where(blk >= th, 1.0, 0.0))
                return c8
            return jnp.sum(lax.fori_loop(0, ntp // 2, body, z8), axis=0, keepdims=True)

        def any_open(done):
            return jnp.max(jnp.where(done, 0.0, 1.0)) > 0.0

        def settled(cl_):
            return jnp.logical_or(jnp.logical_or(small, zero_tie), cl_ == kf)

        def bisect(lo, hi, cl):
            mid = 0.5 * lo + 0.5 * hi
            c = count_ge(mid)
            ge = c >= kf
            return jnp.where(ge, mid, lo), jnp.where(ge, hi, mid), jnp.where(ge, c, cl)

        def b_cond(st):
            it, _, _, _, open_ = st
            return jnp.logical_and(it < BISECT_MAX_ITERS, open_)

        def b_body(st):
            it, lo, hi, cl, _ = st
            lo, hi, cl = bisect(lo, hi, cl)
            return it + 1, lo, hi, cl, any_open(settled(cl))

        open1 = any_open(settled(cl1))
        n_blind = jnp.where(open1, BISECT_BLIND_ITERS, 0)
        lo, hi, cl = lax.fori_loop(0, n_blind, lambda _, st: bisect(*st), (lo1, hi1, cl1))
        _, lo, hi, cl, open_ = lax.while_loop(
            b_cond, b_body, (n_blind, lo, hi, cl, jnp.logical_and(open1, any_open(settled(cl)))))
        has_ties = jnp.logical_or(open_, jnp.max(jnp.where(
            jnp.logical_and(zero_tie, jnp.logical_not(small)), 1.0, 0.0)) > 0.0)

        def fast_mask():
            thr = jnp.where(small, F32_MIN, lo)

            def body(kt, _):
                blk = sc_ref[tile_rows(kt), :]
                sc_ref[tile_rows(kt), :] = jnp.where(blk >= thr, 0.0, NEG)
                return 0
            lax.fori_loop(0, nt, body, 0)

        def exact_mask():
            def done_of(lo_, hi_, cl_):
                return jnp.logical_or(jnp.logical_or(small, cl_ == kf), lo_ >= hi_)

            def s_body(st):
                lo_, hi_, cl_, _ = st
                done = done_of(lo_, hi_, cl_)
                mid = 0.5 * lo_ + 0.5 * hi_
                mid = jnp.where(mid > lo_, mid, hi_)

                def body(kt, c3):
                    c, mn_ge, mx_lt = c3
                    blk = sc_ref[tile_rows(kt), :]
                    ge_ = blk >= mid
                    c = c + jnp.sum(jnp.where(ge_, 1.0, 0.0), axis=0, keepdims=True)
                    mn_ge = jnp.minimum(mn_ge, jnp.min(jnp.where(ge_, blk, jnp.inf),
                                                       axis=0, keepdims=True))
                    mx_lt = jnp.maximum(mx_lt, jnp.max(jnp.where(ge_, -jnp.inf, blk),
                                                       axis=0, keepdims=True))
                    return c, mn_ge, mx_lt
                c, mn_ge, mx_lt = lax.fori_loop(
                    0, nt, body, (zrow, jnp.full((1, T), jnp.inf, F32),
                                  jnp.full((1, T), -jnp.inf, F32)))
                ge = c >= kf
                lo_n = jnp.where(done, lo_, jnp.where(ge, mn_ge, lo_))
                cl_n = jnp.where(done, cl_, jnp.where(ge, c, cl_))
                hi_n = jnp.where(done, hi_, jnp.where(ge, hi_, mx_lt))
                return lo_n, hi_n, cl_n, any_open(done_of(lo_n, hi_n, cl_n))

            lo_e, _, cl_e, _ = lax.while_loop(
                lambda st: st[3], s_body, (lo, hi, cl, any_open(done_of(lo, hi, cl))))
            thr = jnp.where(small, F32_MIN, lo_e)

            def count_gt():
                def body(kt, c8):
                    blk = sc_ref[tile_rows(kt), :]
                    return c8 + fold8(jnp.where(blk > thr, 1.0, 0.0))
                return jnp.sum(lax.fori_loop(0, nt, body, z8), axis=0, keepdims=True)
            cgt = lax.cond(open_, count_gt, lambda: cgt0)
            need = jnp.where(jnp.logical_or(small, cl_e == kf), jnp.inf, kf - cgt)

            UNDECIDED = -1.0

            def m_body(kt, st):
                carry, xt, xc = st
                blk = sc_ref[tile_rows(kt), :]
                eq = jnp.where(blk == thr, 1.0, 0.0)
                cnt = jnp.sum(fold8(eq), axis=0, keepdims=True)
                has_budget = carry < need
                crossing = jnp.logical_and(has_budget, carry + cnt > need)
                tie_val = jnp.where(crossing, UNDECIDED, jnp.where(has_budget, 0.0, NEG))
                sc_ref[tile_rows(kt), :] = jnp.where(
                    blk > thr, 0.0, jnp.where(blk == thr, tie_val, NEG))
                xt = jnp.where(crossing, kt.astype(F32), xt)
                xc = jnp.where(crossing, carry, xc)
                return carry + cnt, xt, xc
            _, xt, xc = lax.fori_loop(0, nt, m_body, (zrow, jnp.full((1, T), -1.0, F32), zrow))

            def r_body(st):
                xt_, ktf = st
                kt = ktf.astype(jnp.int32)
                msk = sc_ref[tile_rows(kt), :]
                und = msk == UNDECIDED
                r_io = lax.broadcasted_iota(jnp.int32, (T, T), 0)
                c_io = lax.broadcasted_iota(jnp.int32, (T, T), 1)
                ltri = jnp.where(c_io < r_io, 1.0, 0.0).astype(BF16)
                rank = jnp.dot(ltri, jnp.where(und, 1.0, 0.0).astype(BF16),
                               preferred_element_type=F32) + xc
                sc_ref[tile_rows(kt), :] = jnp.where(
                    und, jnp.where(rank < need, 0.0, NEG), msk)
                xt_ = jnp.where(xt_ == ktf, -1.0, xt_)
                return xt_, jnp.max(xt_)
            lax.while_loop(lambda st: st[1] >= 0.0, r_body, (xt, jnp.max(xt)))

        lax.cond(has_ties, exact_mask, fast_mask)

        m_ref[...] = jnp.full(m_ref.shape, NEG, F32)
        acc_ref[...] = jnp.zeros(acc_ref.shape, F32)
        vx_ref[HEAD_DIM:, :] = jnp.ones((VX_ROWS - HEAD_DIM, T), BF16)

        def step(kt_a, bias_slab, kt_bc):
            if kt_bc is not None:
                vx_ref[0:HEAD_DIM, :] = vbT_ref[kt_bc]
                m_old = m_ref[...]
                m_new = jnp.maximum(m_old, tmax_ref[...])
                alpha = jnp.exp2(m_old - m_new)
                m_ref[...] = m_new
            if kt_a is not None:
                kb_t = kb_ref[tile_rows(kt_a), :]
                msk = sc_ref[tile_rows(kt_a), :]
            tmax = []
            for h in range(HEADS):
                rows = slice(HEAD_DIM * h, HEAD_DIM * (h + 1))
                if kt_bc is not None:
                    p_ref[h] = jnp.exp2(s_ref[h] - m_new[h:h + 1, :]).astype(BF16)
                if kt_a is not None:
                    add = msk if bias_slab is None else msk + gbias_ref[bias_slab, h]
                    s = jnp.dot(kb_t, qbT_ref[rows, :], preferred_element_type=F32) + add
                    s_ref[h] = s
                    tmax.append(jnp.max(s, axis=0, keepdims=True))
                if kt_bc is not None:
                    for hc in ([h - PV_LAG] if h < HEADS - 1 else range(h - PV_LAG, HEADS)):
                        if hc < 0:
                            continue
                        acc_ref[hc] = alpha[hc:hc + 1, :] * acc_ref[hc] + jnp.dot(
                            vx_ref[...], p_ref[hc], preferred_element_type=F32)
            if kt_a is not None:
                tmax_ref[...] = jnp.concatenate(tmax, axis=0)

        n_far = jnp.maximum(j - 1, 0)
        step(0, jnp.clip(2 - j, 0, 2), None)

        def far_step(a, _):
            step(a, None, a - 1)
            return 0

        def near_step(a, _):
            step(a, a - j + 2, a - 1)
            return 0

        lax.fori_loop(1, n_far, far_step, 0)
        lax.fori_loop(jnp.maximum(n_far, 1), nt, near_step, 0)
        step(None, None, nt - 1)

        outs = []
        for h in range(HEADS):
            a = acc_ref[h]
            outs.append(a[0:HEAD_DIM, :] * (1.0 / a[HEAD_DIM:HEAD_DIM + 1, :]))
        o_ref[...] = jnp.concatenate(outs, axis=0).T

    return kernel


def _dsa(qbT, qiT, wiT, ki3, kb3, vbT4, gbias, adm, n_sel):
    b, s, _ = ki3.shape
    T = DSA_T
    nq = s // T
    grid = (b, nq)
    in_specs = [
        pl.BlockSpec((WIDTH, T), lambda bi, j: (0, bi * nq + j)),
        pl.BlockSpec((WIDTH, T), lambda bi, j: (0, bi * nq + j)),
        pl.BlockSpec((HEADS, T), lambda bi, j: (0, bi * nq + j)),
        pl.BlockSpec((None, s, HEAD_DIM), lambda bi, j: (bi, 0, 0)),
        pl.BlockSpec((None, s, HEAD_DIM), lambda bi, j: (bi, 0, 0)),
        pl.BlockSpec((None, nq, HEAD_DIM, T), lambda bi, j: (bi, 0, 0, 0)),
        pl.BlockSpec(gbias.shape, lambda bi, j: (0, 0, 0, 0)),
        pl.BlockSpec(adm.shape, lambda bi, j: (0, 0, 0)),
    ]
    out_specs = pl.BlockSpec((None, T, WIDTH), lambda bi, j: (bi, j, 0))
    scratch = [
        pltpu.VMEM((s + T, T), F32),
        pltpu.VMEM((HEADS, T), F32),
        pltpu.VMEM((HEADS, VX_ROWS, T), F32),
        pltpu.VMEM((HEADS, T), F32),
        pltpu.VMEM((HEADS, T, T), F32),
        pltpu.VMEM((HEADS, T, T), BF16),
        pltpu.VMEM((VX_ROWS, T), BF16),
    ]
    blk = (2 * (2 * _nbytes((WIDTH, T), BF16) + _nbytes((HEADS, T), F32)
                + 2 * _nbytes((s, V7X_LANES), BF16) + _nbytes((HEAD_DIM, s), BF16)
                + _nbytes(gbias.shape, F32) + _nbytes(adm.shape, F32)
                + _nbytes((T, WIDTH), F32))
           + _nbytes((s + T, T), F32) + 2 * _nbytes((HEADS, T), F32)
           + _nbytes((HEADS, VX_ROWS, T), F32) + _nbytes((HEADS, T, T), F32)
           + _nbytes((HEADS, T, T), BF16) + _nbytes((VX_ROWS, T), BF16))
    return pl.pallas_call(
        _make_dsa_kernel(n_sel), out_shape=jax.ShapeDtypeStruct((b, s, WIDTH), F32),
        grid=grid, in_specs=in_specs, out_specs=out_specs, scratch_shapes=scratch, name="dsa",
        compiler_params=pltpu.CompilerParams(
            dimension_semantics=("arbitrary", "arbitrary"), vmem_limit_bytes=_vmem_limit(blk)),
    )(qbT, qiT, wiT, ki3, kb3, vbT4, gbias, adm)


def _merge_kernel(x_ref, ya_ref, yb_ref, gate_ref, wa_ref, wb_ref, wo_ref, fg_ref, o_ref):
    d = x_ref.shape[1]
    za = gate_ref[:, 0:WIDTH].astype(F32)
    zb = gate_ref[:, WIDTH:2 * WIDTH].astype(F32)
    ga = gate_ref[:, 2 * WIDTH:2 * WIDTH + d].astype(F32)
    gb = gate_ref[:, 2 * WIDTH + d:2 * WIDTH + 2 * d].astype(F32)
    ua = (ya_ref[...] * (za * _sigmoid(za))).astype(BF16)
    ub = (yb_ref[...] * (zb * _sigmoid(zb))).astype(BF16)
    pa = jnp.dot(ua, wa_ref[...], preferred_element_type=F32)
    pb = jnp.dot(ub, wb_ref[...], preferred_element_type=F32)
    merged = _sigmoid(ga) * pa + _sigmoid(gb) * pb
    h = x_ref[...] + jnp.dot(merged.astype(BF16), wo_ref[...], preferred_element_type=F32)
    ms = jnp.mean(h * h, axis=-1, keepdims=True)
    o_ref[...] = (h * lax.rsqrt(ms + EPS)) * fg_ref[...]


def _merge(x2, ya2, yb2, gates, wa, wb, wo, fg, tm):
    n, d = x2.shape
    n_gate = gates.shape[1]
    grid = (n // tm,)
    row = lambda i: (i, 0)
    const = lambda i: (0, 0)
    in_specs = [
        pl.BlockSpec((tm, d), row),
        pl.BlockSpec((tm, WIDTH), row),
        pl.BlockSpec((tm, WIDTH), row),
        pl.BlockSpec((tm, n_gate), row),
        pl.BlockSpec(wa.shape, const),
        pl.BlockSpec(wb.shape, const),
        pl.BlockSpec(wo.shape, const),
        pl.BlockSpec((1, d), const),
    ]
    blk = 2 * (2 * _nbytes((tm, d), F32) + 2 * _nbytes((tm, WIDTH), F32)
               + _nbytes((tm, n_gate), BF16) + _nbytes(wa.shape, BF16) + _nbytes(wb.shape, BF16)
               + _nbytes(wo.shape, BF16))
    return pl.pallas_call(
        _merge_kernel, out_shape=jax.ShapeDtypeStruct((n, d), F32),
        grid=grid, in_specs=in_specs, out_specs=pl.BlockSpec((tm, d), row), name="merge",
        compiler_params=pltpu.CompilerParams(
            dimension_semantics=("arbitrary",), vmem_limit_bytes=_vmem_limit(blk)),
    )(x2, ya2, yb2, gates, wa, wb, wo, fg)


def _toeplitz(g, nrows, ncols):
    period = nrows + ncols
    lead = g.shape[:-1]
    p = jnp.concatenate([g[..., ::-1], jnp.zeros(lead + (1,), g.dtype)], axis=-1)
    flat = jnp.tile(p, (1,) * len(lead) + (nrows,))[..., :nrows * (period - 1)]
    x = flat.reshape(lead + (nrows, period - 1))
    return x[..., nrows - 1:nrows - 1 + ncols]


def _band_bias(a_rel_bias):
    pad = A_LEFT_CHUNKS * CHUNK
    rel = np.arange(BAND_TQ + BAND_WIN - 1) - (BAND_WIN - 1) + pad
    idx = np.clip(rel, -A_REL_CLIP, A_REL_CLIP) + A_REL_CLIP
    lo, hi = int(idx[0]), int(idx.max())
    n_flat = int((idx == hi).sum()) - 1
    assert np.array_equal(idx, np.minimum(np.arange(lo, lo + idx.size), hi))
    ab = a_rel_bias.astype(F32)
    g = jnp.concatenate([ab[:, lo:hi + 1], jnp.broadcast_to(ab[:, hi:hi + 1], (HEADS, n_flat))],
                        axis=1)
    bias = _toeplitz(g[:, ::-1], BAND_WIN, BAND_TQ)
    jj = np.arange(BAND_WIN)[:, None]
    qq = np.arange(BAND_TQ)[None, :]
    in_band = np.logical_and(jj // CHUNK >= qq // CHUNK, jj // CHUNK <= qq // CHUNK + A_LEFT_CHUNKS)
    return jnp.where(jnp.asarray(in_band)[None], bias * LOG2E, NEG)


def _dsa_bias(t5_bias):
    T = DSA_T
    far = T5_BUCKETS // 2 - 1
    tb = t5_bias.astype(F32)
    slabs = [jnp.zeros((HEADS, T, T), F32)]
    for off in (-T, 0):
        rel = jnp.arange(2 * T - 1, dtype=jnp.int32) - (T - 1) + off
        g = (tb[_t5_bucket(rel)] - tb[far]) * LOG2E
        slabs.append(_toeplitz(g.T, T, T))
    return jnp.stack(slabs)


def _far_bucket_is_constant(s):
    half = T5_BUCKETS // 2
    max_exact = half // 2
    for n in (DSA_T + 1, max(s - 1, DSA_T + 1)):
        large = max_exact + int(math.log(n / max_exact) / math.log(T5_MAX_DIST / max_exact)
                                * (half - max_exact))
        if min(large, half - 1) != half - 1:
            return False
    return True


def kernel(x, norm_gain, w_in, a_rel_bias, t5_bias, w_a_out, w_b_out, w_out, final_gain):
    b, s, d = x.shape
    depth = w_in.shape[0]
    n = b * s
    assert s % DSA_T == 0 and s % BAND_TQ == 0
    assert _far_bucket_is_constant(s)
    tm = 512 if n % 512 == 0 else DSA_T
    n_sel = min(TOPK_MAX, s // 4)

    splits = (WIDTH,) * 4 + (WIDTH, HEAD_DIM, HEAD_DIM, WIDTH) + (WIDTH, HEAD_DIM, HEADS) + (d, d)
    offs = np.concatenate([[0], np.cumsum(splits)])
    assert offs[-1] == w_in.shape[2]

    abias = _band_bias
    gbias = _dsa_bias(t5_bias)
    kk = jnp.arange(DSA_T)[:, None] // CHUNK
    qq = jnp.arange(DSA_T)[None, :] // CHUNK
    adm = jnp.stack([jnp.zeros((DSA_T, DSA_T), F32),
                     jnp.where(kk <= qq, 0.0, -jnp.inf).astype(F32),
                     jnp.full((DSA_T, DSA_T), -jnp.inf, F32)])

    h2 = x.reshape(n, d)
    for l in range(depth):
        w = w_in[l]
        (qa, ka, va, za, qb, kb, vb, zb, qi, ki, wi, ga, gb) = [
            w[:, offs[i]:offs[i + 1]] for i in range(len(splits))]
        wn = jnp.concatenate([ka, ki, kb], axis=1).astype(BF16)
        wg = jnp.concatenate([za, zb, ga, gb], axis=1).astype(BF16)
        qscale = HEAD_DIM ** -0.5 * LOG2E
        wt = jnp.concatenate([qa * qscale, va, qb * qscale, qi, vb, wi,
                              jnp.zeros((d, HEADS), w.dtype)], axis=1).T.astype(BF16)

        ka_o, ki_o, kb_o, gates, qaT, vaT, qbT, qiT, vbT, wiT = _proj(
            h2, norm_gain[l].reshape(1, d).astype(F32), wn, wg, wt, tm)

        ya = _band(qaT, ka_o.reshape(b, s, WIDTH),
                   vaT.reshape(b, s // BAND_KB, WIDTH, BAND_KB), abias(a_rel_bias[l]))
        yb = _dsa(qbT, qiT, wiT, ki_o.reshape(b, s, HEAD_DIM), kb_o.reshape(b, s, HEAD_DIM),
                  vbT.reshape(b, s // DSA_T, HEAD_DIM, DSA_T), gbias, adm, n_sel)

        assert depth == 1
        h2 = _merge(h2, ya.reshape(n, WIDTH), yb.reshape(n, WIDTH), gates,
                    w_a_out[l].astype(BF16), w_b_out[l].astype(BF16), w_out[l].astype(BF16),
                    final_gain.reshape(1, d).astype(F32), tm)
    return h2.reshape(b, s, d)
```

```python
import math

import numpy as np
import jax
import jax.numpy as jnp
from jax import lax
from jax.experimental import pallas as pl
from jax.experimental.pallas import tpu as pltpu

F32 = jnp.float32
BF16 = jnp.bfloat16

CHUNK = 64
EPS = 1e-6
HEADS = 8
HEAD_DIM = 64
WIDTH = HEADS * HEAD_DIM
A_LEFT_CHUNKS = 8
A_REL_CLIP = 256
TOPK_MAX = 256
T5_BUCKETS = 32
T5_MAX_DIST = 128
NEG = -1e30
F32_MIN = float(np.finfo(np.float32).min)
LOG2E = math.log2(math.e)

V7X_LANES = 128
V7X_SUBLANES = 8
VMEM_CAP_BYTES = 60000 * 1024
COMPILER_TEMP_BYTES = 8 * 1024 * 1024

BAND_TQ = 4 * CHUNK
BAND_WIN = A_LEFT_CHUNKS * CHUNK + BAND_TQ
BAND_KB = V7X_LANES
BAND_NBLK = BAND_WIN // BAND_KB
DSA_T = 256
VX_ROWS = HEAD_DIM + 2 * V7X_SUBLANES
PV_LAG = 2
BISECT_MAX_ITERS = 26
BISECT_BLIND_ITERS = 16


def _vmem_limit(block_bytes):
    return int(min(block_bytes + COMPILER_TEMP_BYTES, VMEM_CAP_BYTES))


def _nbytes(shape, dtype):
    return int(np.prod(shape)) * jnp.dtype(dtype).itemsize


def _t5_bucket(rel):
    half = T5_BUCKETS // 2
    max_exact = half // 2
    ret = jnp.where(rel > 0, half, 0)
    n = jnp.abs(rel)
    nf = jnp.maximum(n, 1).astype(F32)
    large = max_exact + (jnp.log(nf / max_exact) / math.log(T5_MAX_DIST / max_exact)
                         * (half - max_exact)).astype(jnp.int32)
    large = jnp.minimum(large, half - 1)
    return ret + jnp.where(n < max_exact, n, large)


def _sigmoid(v):
    return 1.0 / (1.0 + jnp.exp(-v))


def _proj_kernel(x_ref, g_ref, wn_ref, wg_ref, wt_ref,
                 ka_ref, ki_ref, kb_ref, gate_ref,
                 qaT_ref, vaT_ref, qbT_ref, qiT_ref, vbT_ref, wiT_ref):
    xf = x_ref[...]
    ms = jnp.mean(xf * xf, axis=-1, keepdims=True)
    hn = ((xf * lax.rsqrt(ms + EPS)) * g_ref[...]).astype(BF16)
    tm = hn.shape[0]

    ka_ref[...] = jnp.dot(hn, wn_ref[:, 0:WIDTH], preferred_element_type=F32).astype(BF16)
    kk = jnp.dot(hn, wn_ref[:, WIDTH:WIDTH + 2 * HEAD_DIM], preferred_element_type=F32)
    ki_ref[...] = kk[:, :HEAD_DIM].astype(BF16)
    kb_ref[...] = kk[:, HEAD_DIM:].astype(BF16)
    n_gate = gate_ref.shape[1]
    for c0 in range(0, n_gate, WIDTH):
        gate_ref[:, c0:c0 + WIDTH] = jnp.dot(
            hn, wg_ref[:, c0:c0 + WIDTH], preferred_element_type=F32).astype(BF16)

    def t_rows(r0, nrows):
        return lax.dot_general(wt_ref[r0:r0 + nrows, :], hn, (((1,), (1,)), ((), ())),
                               preferred_element_type=F32)
    qaT_ref[...] = t_rows(0, WIDTH).astype(BF16)
    vaT = t_rows(WIDTH, WIDTH).astype(BF16)
    for s in range(tm // BAND_KB):
        vaT_ref[s] = vaT[:, s * BAND_KB:(s + 1) * BAND_KB]
    qbT_ref[...] = t_rows(2 * WIDTH, WIDTH).astype(BF16)
    qiT_ref[...] = t_rows(3 * WIDTH, WIDTH).astype(BF16)
    tail = t_rows(4 * WIDTH, HEAD_DIM + 2 * HEADS)
    for s in range(tm // DSA_T):
        vbT_ref[s] = tail[0:HEAD_DIM, s * DSA_T:(s + 1) * DSA_T].astype(BF16)
    wiT_ref[...] = tail[HEAD_DIM:HEAD_DIM + HEADS, :]


def _proj(x2, gain, wn, wg, wt, tm):
    n, d = x2.shape
    n_gate = wg.shape[1]
    grid = (n // tm,)
    const = lambda i: (0, 0)
    row = lambda i: (i, 0)
    col = lambda i: (0, i)
    in_specs = [
        pl.BlockSpec((tm, d), row),
        pl.BlockSpec((1, d), const),
        pl.BlockSpec(wn.shape, const),
        pl.BlockSpec(wg.shape, const),
        pl.BlockSpec(wt.shape, const),
    ]
    out_shape = (
        jax.ShapeDtypeStruct((n, WIDTH), BF16),
        jax.ShapeDtypeStruct((n, HEAD_DIM), BF16),
        jax.ShapeDtypeStruct((n, HEAD_DIM), BF16),
        jax.ShapeDtypeStruct((n, n_gate), BF16),
        jax.ShapeDtypeStruct((WIDTH, n), BF16),
        jax.ShapeDtypeStruct((n // BAND_KB, WIDTH, BAND_KB), BF16),
        jax.ShapeDtypeStruct((WIDTH, n), BF16),
        jax.ShapeDtypeStruct((WIDTH, n), BF16),
        jax.ShapeDtypeStruct((n // DSA_T, HEAD_DIM, DSA_T), BF16),
        jax.ShapeDtypeStruct((HEADS, n), F32),
    )
    out_specs = (
        pl.BlockSpec((tm, WIDTH), row),
        pl.BlockSpec((tm, HEAD_DIM), row),
        pl.BlockSpec((tm, HEAD_DIM), row),
        pl.BlockSpec((tm, n_gate), row),
        pl.BlockSpec((WIDTH, tm), col),
        pl.BlockSpec((tm // BAND_KB, WIDTH, BAND_KB), lambda i: (i, 0, 0)),
        pl.BlockSpec((WIDTH, tm), col),
        pl.BlockSpec((WIDTH, tm), col),
        pl.BlockSpec((tm // DSA_T, HEAD_DIM, DSA_T), lambda i: (i, 0, 0)),
        pl.BlockSpec((HEADS, tm), col),
    )
    blk = 2 * (_nbytes((tm, d), F32) + _nbytes(wn.shape, BF16) + _nbytes(wg.shape, BF16)
               + _nbytes(wt.shape, BF16) + _nbytes((tm, WIDTH), BF16)
               + 2 * _nbytes((tm, V7X_LANES), BF16) + _nbytes((tm, n_gate), BF16)
               + 4 * _nbytes((WIDTH, tm), BF16) + _nbytes((HEAD_DIM, tm), BF16)
               + _nbytes((HEADS, tm), F32))
    return pl.pallas_call(
        _proj_kernel, out_shape=out_shape, grid=grid, in_specs=in_specs, out_specs=out_specs,
        name="proj",
        compiler_params=pltpu.CompilerParams(
            dimension_semantics=("arbitrary",), vmem_limit_bytes=_vmem_limit(blk)),
    )(x2, gain, wn, wg, wt)


def _band_kernel(qT_ref, k_ref, vT_ref, bias_ref, o_ref, s_ref, p_ref):
    t = pl.program_id(1)
    row = lax.broadcasted_iota(jnp.int32, (V7X_LANES, BAND_TQ), 0)
    first_half = row < HEAD_DIM
    n_left = A_LEFT_CHUNKS * CHUNK // BAND_KB
    blocks = []
    for i in range(BAND_NBLK):
        kidx = t * (BAND_TQ // BAND_KB) - n_left + i
        kc = jnp.maximum(kidx, 0)
        blocks.append((kidx, kc, pl.multiple_of(kc * BAND_KB, BAND_KB)))

    def stage_a(h):
        cs = slice(V7X_LANES * (h // 2), V7X_LANES * (h // 2 + 1))
        qT = qT_ref[cs, :]
        qe = jnp.where(first_half if h % 2 == 0 else jnp.logical_not(first_half), qT,
                       jnp.zeros_like(qT))
        m_el = None
        for i, (kidx, kc, ks) in enumerate(blocks):
            ksl = slice(BAND_KB * i, BAND_KB * (i + 1))
            s = jnp.dot(k_ref[pl.ds(ks, BAND_KB), cs], qe, preferred_element_type=F32)
            s = s + bias_ref[h, ksl, :]
            if i < n_left:
                s = jnp.where(kidx >= 0, s, NEG)
            s_ref[h, ksl, :] = s
            m_el = s if m_el is None else jnp.maximum(m_el, s)
        return jnp.max(m_el, axis=0, keepdims=True)

    ones = jnp.ones((VX_ROWS - HEAD_DIM, BAND_KB), BF16)

    def stage_bc(h, m_h):
        p_ref[h] = jnp.exp2(s_ref[h] - m_h).astype(BF16)
        acc = None
        for i, (kidx, kc, ks) in enumerate(blocks):
            vx = jnp.concatenate([vT_ref[kc, HEAD_DIM * h:HEAD_DIM * (h + 1), :], ones], axis=0)
            pv = jnp.dot(vx, p_ref[h, BAND_KB * i:BAND_KB * (i + 1), :],
                         preferred_element_type=F32)
            acc = pv if acc is None else acc + pv
        return acc[0:HEAD_DIM, :] * (1.0 / acc[HEAD_DIM:HEAD_DIM + 1, :])

    m = [stage_a(h) for h in range(HEADS)]
    outs = [stage_bc(h, m[h]) for h in range(HEADS)]
    o_ref[...] = jnp.concatenate(outs, axis=0).T


def _band(qaT, ka3, vaT4, abiasT):
    b, s, _ = ka3.shape
    nt = s // BAND_TQ
    grid = (b, nt)
    once = pl.Buffered(1)
    in_specs = [
        pl.BlockSpec((WIDTH, BAND_TQ), lambda bi, t: (0, bi * nt + t)),
        pl.BlockSpec((None, s, WIDTH), lambda bi, t: (bi, 0, 0), pipeline_mode=once),
        pl.BlockSpec((None, s // BAND_KB, WIDTH, BAND_KB), lambda bi, t: (bi, 0, 0, 0),
                     pipeline_mode=once),
        pl.BlockSpec(abiasT.shape, lambda bi, t: (0, 0, 0), pipeline_mode=once),
    ]
    out_specs = pl.BlockSpec((None, BAND_TQ, WIDTH), lambda bi, t: (bi, t, 0))
    scratch = [
        pltpu.VMEM((HEADS, BAND_WIN, BAND_TQ), F32),
        pltpu.VMEM((HEADS, BAND_WIN, BAND_TQ), BF16),
    ]
    blk = (2 * (_nbytes((WIDTH, BAND_TQ), BF16) + _nbytes((BAND_TQ, WIDTH), F32))
           + 2 * _nbytes((s, WIDTH), BF16) + _nbytes(abiasT.shape, F32)
           + _nbytes((HEADS, BAND_WIN, BAND_TQ), F32) + _nbytes((HEADS, BAND_WIN, BAND_TQ), BF16))
    return pl.pallas_call(
        _band_kernel, out_shape=jax.ShapeDtypeStruct((b, s, WIDTH), F32),
        grid=grid, in_specs=in_specs, out_specs=out_specs, scratch_shapes=scratch, name="band",
        compiler_params=pltpu.CompilerParams(
            dimension_semantics=("arbitrary", "arbitrary"), vmem_limit_bytes=_vmem_limit(blk)),
    )(qaT, ka3, vaT4, abiasT)


def _make_dsa_kernel(n_sel):
    T = DSA_T
    kf = float(n_sel)
    idx_scale = (HEADS ** -0.5) * (HEAD_DIM ** -0.5)

    def kernel(qbT_ref, qiT_ref, wiT_ref, ki_ref, kb_ref, vbT_ref, gbias_ref, adm_ref, o_ref,
               sc_ref, m_ref, acc_ref, tmax_ref, s_ref, p_ref, vx_ref):
        j = pl.program_id(1)
        nt = j + 1
        zrow = jnp.zeros((1, T), F32)

        def tile_rows(kt):
            return pl.ds(pl.multiple_of(kt * T, T), T)

        wis = wiT_ref[...] * idx_scale

        def fold8(a):
            return jnp.sum(a.reshape(a.shape[0] // V7X_SUBLANES, V7X_SUBLANES, T), axis=0)

        HT = T // 2

        raw_ref = s_ref.at[0]

        def p1_half(kt_dot, kt_post, half, carry):
            hs = slice(half * HT, (half + 1) * HT)
            if kt_post is not None:
                mn, mx, cgt0, cge0 = carry
                acc = raw_ref[hs, :]
                mn = jnp.minimum(mn, jnp.min(acc, axis=0, keepdims=True))
                mx = jnp.maximum(mx, jnp.max(acc, axis=0, keepdims=True))
                slab = (kt_post == j).astype(jnp.int32) + 2 * (kt_post > j).astype(jnp.int32)
                sc = acc + adm_ref[slab, hs, :]
                sc_ref[pl.ds(pl.multiple_of(kt_post * T + half * HT, HT), HT), :] = sc
                cgt0 = cgt0 + fold8(jnp.where(sc > 0.0, 1.0, 0.0))
                cge0 = cge0 + fold8(jnp.where(sc >= 0.0, 1.0, 0.0))
                carry = (mn, mx, cgt0, cge0)
            if kt_dot is not None:
                kt_ld = jnp.minimum(kt_dot, pl.num_programs(1) - 1)
                ki_t = ki_ref[pl.ds(pl.multiple_of(kt_ld * T + half * HT, HT), HT), :]
                acc = jnp.zeros((HT, T), F32)
                for h in range(HEADS):
                    lg = jnp.dot(ki_t, qiT_ref[HEAD_DIM * h:HEAD_DIM * (h + 1), :],
                                 preferred_element_type=F32)
                    acc = acc + jnp.maximum(lg, 0.0) * wis[h:h + 1, :]
                raw_ref[hs, :] = acc
            return carry

        ntp = nt + nt % 2

        def p1_tile(kt_dot, kt_post, carry):
            for half in range(2):
                carry = p1_half(kt_dot, kt_post, half, carry)
            return carry

        def p1(i, carry):
            carry = p1_tile(2 * i + 1, 2 * i, carry)
            return p1_tile(2 * i + 2, 2 * i + 1, carry)

        z8 = jnp.zeros((V7X_SUBLANES, T), F32)
        carry = (jnp.full((1, T), jnp.inf, F32), jnp.full((1, T), -jnp.inf, F32), z8, z8)
        p1_tile(0, None, None)
        carry = lax.fori_loop(0, ntp // 2 - 1, p1, carry)
        carry = p1_tile(ntp - 1, ntp - 2, carry)
        carry = p1_tile(None, ntp - 1, carry)
        lo0, hi0, cgt0, cge0 = carry
        cgt0 = jnp.sum(cgt0, axis=0, keepdims=True)
        cge0 = jnp.sum(cge0, axis=0, keepdims=True)

        qpos = j * T + lax.broadcasted_iota(jnp.int32, (1, T), 1)
        n_adm = ((qpos // CHUNK + 1) * CHUNK).astype(F32)
        small = n_adm <= kf
        above0 = cgt0 >= kf
        below0 = cge0 < kf
        zero_tie = jnp.logical_not(jnp.logical_or(above0, below0))
        lo_pos = lo0 > 0.0
        lo1 = jnp.where(zero_tie, 0.0, jnp.where(jnp.logical_and(above0, jnp.logical_not(lo_pos)),
                                                 0.0, lo0))
        cl1 = jnp.where(jnp.logical_or(zero_tie, jnp.logical_and(above0, jnp.logical_not(lo_pos))),
                        cge0, n_adm)
        hi1 = jnp.where(zero_tie, 0.0, jnp.where(below0, jnp.minimum(hi0, 0.0), hi0))

        def count_ge(th):
            def body(i, c8):
                for u in range(2):
                    blk = sc_ref[tile_rows(2 * i + u), :]
                    c8 = c8 + fold8(jnp.where(blk >= th, 1.0, 0.0))
                return c8
            return jnp.sum(lax.fori_loop(0, ntp // 2, body, z8), axis=0, keepdims=True)

        def any_open(done):
            return jnp.max(jnp.where(done, 0.0, 1.0)) > 0.0

        def settled(cl_):
            return jnp.logical_or(jnp.logical_or(small, zero_tie), cl_ == kf)

        def bisect(lo, hi, cl):
            mid = 0.5 * lo + 0.5 * hi
            c = count_ge(mid)
            ge = c >= kf
            return jnp.where(ge, mid, lo), jnp.where(ge, hi, mid), jnp.where(ge, c, cl)

        def b_cond(st):
            it, _, _, _, open_ = st
            return jnp.logical_and(it < BISECT_MAX_ITERS, open_)

        def b_body(st):
            it, lo, hi, cl, _ = st
            lo, hi, cl = bisect(lo, hi, cl)
            return it + 1, lo, hi, cl, any_open(settled(cl))

        open1 = any_open(settled(cl1))
        n_blind = jnp.where(open1, BISECT_BLIND_ITERS, 0)
        lo, hi, cl = lax.fori_loop(0, n_blind, lambda _, st: bisect(*st), (lo1, hi1, cl1))
        _, lo, hi, cl, open_ = lax.while_loop(
            b_cond, b_body, (n_blind, lo, hi, cl, jnp.logical_and(open1, any_open(settled(cl)))))
        has_ties = jnp.logical_or(open_, jnp.max(jnp.where(
            jnp.logical_and(zero_tie, jnp.logical_not(small)), 1.0, 0.0)) > 0.0)

        def fast_mask():
            thr = jnp.where(small, F32_MIN, lo)

            def body(kt, _):
                blk = sc_ref[tile_rows(kt), :]
                sc_ref[tile_rows(kt), :] = jnp.where(blk >= thr, 0.0, NEG)
                return 0
            lax.fori_loop(0, nt, body, 0)

        def exact_mask():
            def done_of(lo_, hi_, cl_):
                return jnp.logical_or(jnp.logical_or(small, cl_ == kf), lo_ >= hi_)

            def s_body(st):
                lo_, hi_, cl_, _ = st
                done = done_of(lo_, hi_, cl_)
                mid = 0.5 * lo_ + 0.5 * hi_
                mid = jnp.where(mid > lo_, mid, hi_)

                def body(kt, c3):
                    c, mn_ge, mx_lt = c3
                    blk = sc_ref[tile_rows(kt), :]
                    ge_ = blk >= mid
                    c = c + jnp.sum(jnp.where(ge_, 1.0, 0.0), axis=0, keepdims=True)
                    mn_ge = jnp.minimum(mn_ge, jnp.min(jnp.where(ge_, blk, jnp.inf),
                                                       axis=0, keepdims=True))
                    mx_lt = jnp.maximum(mx_lt, jnp.max(jnp.where(ge_, -jnp.inf, blk),
                                                       axis=0, keepdims=True))
                    return c, mn_ge, mx_lt
                c, mn_ge, mx_lt = lax.fori_loop(
                    0, nt, body, (zrow, jnp.full((1, T), jnp.inf, F32),
                                  jnp.full((1, T), -jnp.inf, F32)))
                ge = c >= kf
                lo_n = jnp.where(done, lo_, jnp.where(ge, mn_ge, lo_))
                cl_n = jnp.where(done, cl_, jnp.where(ge, c, cl_))
                hi_n = jnp.where(done, hi_, jnp.where(ge, hi_, mx_lt))
                return lo_n, hi_n, cl_n, any_open(done_of(lo_n, hi_n, cl_n))

            lo_e, _, cl_e, _ = lax.while_loop(
                lambda st: st[3], s_body, (lo, hi, cl, any_open(done_of(lo, hi, cl))))
            thr = jnp.where(small, F32_MIN, lo_e)

            def count_gt():
                def body(kt, c8):
                    blk = sc_ref[tile_rows(kt), :]
                    return c8 + fold8(jnp.where(blk > thr, 1.0, 0.0))
                return jnp.sum(lax.fori_loop(0, nt, body, z8), axis=0, keepdims=True)
            cgt = lax.cond(open_, count_gt, lambda: cgt0)
            need = jnp.where(jnp.logical_or(small, cl_e == kf), jnp.inf, kf - cgt)

            UNDECIDED = -1.0

            def m_body(kt, st):
                carry, xt, xc = st
                blk = sc_ref[tile_rows(kt), :]
                eq = jnp.where(blk == thr, 1.0, 0.0)
                cnt = jnp.sum(fold8(eq), axis=0, keepdims=True)
                has_budget = carry < need
                crossing = jnp.logical_and(has_budget, carry + cnt > need)
                tie_val = jnp.where(crossing, UNDECIDED, jnp.where(has_budget, 0.0, NEG))
                sc_ref[tile_rows(kt), :] = jnp.where(
                    blk > thr, 0.0, jnp.where(blk == thr, tie_val, NEG))
                xt = jnp.where(crossing, kt.astype(F32), xt)
                xc = jnp.where(crossing, carry, xc)
                return carry + cnt, xt, xc
            _, xt, xc = lax.fori_loop(0, nt, m_body, (zrow, jnp.full((1, T), -1.0, F32), zrow))

            def r_body(st):
                xt_, ktf = st
                kt = ktf.astype(jnp.int32)
                msk = sc_ref[tile_rows(kt), :]
                und = msk == UNDECIDED
                r_io = lax.broadcasted_iota(jnp.int32, (T, T), 0)
                c_io = lax.broadcasted_iota(jnp.int32, (T, T), 1)
                ltri = jnp.where(c_io < r_io, 1.0, 0.0).astype(BF16)
                rank = jnp.dot(ltri, jnp.where(und, 1.0, 0.0).astype(BF16),
                               preferred_element_type=F32) + xc
                sc_ref[tile_rows(kt), :] = jnp.where(
                    und, jnp.where(rank < need, 0.0, NEG), msk)
                xt_ = jnp.where(xt_ == ktf, -1.0, xt_)
                return xt_, jnp.max(xt_)
            lax.while_loop(lambda st: st[1] >= 0.0, r_body, (xt, jnp.max(xt)))

        lax.cond(has_ties, exact_mask, fast_mask)

        m_ref[...] = jnp.full(m_ref.shape, NEG, F32)
        acc_ref[...] = jnp.zeros(acc_ref.shape, F32)
        vx_ref[HEAD_DIM:, :] = jnp.ones((VX_ROWS - HEAD_DIM, T), BF16)

        def step(kt_a, bias_slab, kt_bc):
            if kt_bc is not None:
                vx_ref[0:HEAD_DIM, :] = vbT_ref[kt_bc]
                m_old = m_ref[...]
                m_new = jnp.maximum(m_old, tmax_ref[...])
                alpha = jnp.exp2(m_old - m_new)
                m_ref[...] = m_new
            if kt_a is not None:
                kb_t = kb_ref[tile_rows(kt_a), :]
                msk = sc_ref[tile_rows(kt_a), :]
            tmax = []
            for h in range(HEADS):
                rows = slice(HEAD_DIM * h, HEAD_DIM * (h + 1))
                if kt_bc is not None:
                    p_ref[h] = jnp.exp2(s_ref[h] - m_new[h:h + 1, :]).astype(BF16)
                if kt_a is not None:
                    add = msk if bias_slab is None else msk + gbias_ref[bias_slab, h]
                    s = jnp.dot(kb_t, qbT_ref[rows, :], preferred_element_type=F32) + add
                    s_ref[h] = s
                    tmax.append(jnp.max(s, axis=0, keepdims=True))
                if kt_bc is not None:
                    for hc in ([h - PV_LAG] if h < HEADS - 1 else range(h - PV_LAG, HEADS)):
                        if hc < 0:
                            continue
                        acc_ref[hc] = alpha[hc:hc + 1, :] * acc_ref[hc] + jnp.dot(
                            vx_ref[...], p_ref[hc], preferred_element_type=F32)
            if kt_a is not None:
                tmax_ref[...] = jnp.concatenate(tmax, axis=0)

        n_far = jnp.maximum(j - 1, 0)
        step(0, jnp.clip(2 - j, 0, 2), None)

        def far_step(a, _):
            step(a, None, a - 1)
            return 0

        def near_step(a, _):
            step(a, a - j + 2, a - 1)
            return 0

        lax.fori_loop(1, n_far, far_step, 0)
        lax.fori_loop(jnp.maximum(n_far, 1), nt, near_step, 0)
        step(None, None, nt - 1)

        outs = []
        for h in range(HEADS):
            a = acc_ref[h]
            outs.append(a[0:HEAD_DIM, :] * (1.0 / a[HEAD_DIM:HEAD_DIM + 1, :]))
        o_ref[...] = jnp.concatenate(outs, axis=0).T

    return kernel


def _dsa(qbT, qiT, wiT, ki3, kb3, vbT4, gbias, adm, n_sel):
    b, s, _ = ki3.shape
    T = DSA_T
    nq = s // T
    grid = (b, nq)
    in_specs = [
        pl.BlockSpec((WIDTH, T), lambda bi, j: (0, bi * nq + j)),
        pl.BlockSpec((WIDTH, T), lambda bi, j: (0, bi * nq + j)),
        pl.BlockSpec((HEADS, T), lambda bi, j: (0, bi * nq + j)),
        pl.BlockSpec((None, s, HEAD_DIM), lambda bi, j: (bi, 0, 0)),
        pl.BlockSpec((None, s, HEAD_DIM), lambda bi, j: (bi, 0, 0)),
        pl.BlockSpec((None, nq, HEAD_DIM, T), lambda bi, j: (bi, 0, 0, 0)),
        pl.BlockSpec(gbias.shape, lambda bi, j: (0, 0, 0, 0)),
        pl.BlockSpec(adm.shape, lambda bi, j: (0, 0, 0)),
    ]
    out_specs = pl.BlockSpec((None, T, WIDTH), lambda bi, j: (bi, j, 0))
    scratch = [
        pltpu.VMEM((s + T, T), F32),
        pltpu.VMEM((HEADS, T), F32),
        pltpu.VMEM((HEADS, VX_ROWS, T), F32),
        pltpu.VMEM((HEADS, T), F32),
        pltpu.VMEM((HEADS, T, T), F32),
        pltpu.VMEM((HEADS, T, T), BF16),
        pltpu.VMEM((VX_ROWS, T), BF16),
    ]
    blk = (2 * (2 * _nbytes((WIDTH, T), BF16) + _nbytes((HEADS, T), F32)
                + 2 * _nbytes((s, V7X_LANES), BF16) + _nbytes((HEAD_DIM, s), BF16)
                + _nbytes(gbias.shape, F32) + _nbytes(adm.shape, F32)
                + _nbytes((T, WIDTH), F32))
           + _nbytes((s + T, T), F32) + 2 * _nbytes((HEADS, T), F32)
           + _nbytes((HEADS, VX_ROWS, T), F32) + _nbytes((HEADS, T, T), F32)
           + _nbytes((HEADS, T, T), BF16) + _nbytes((VX_ROWS, T), BF16))
    return pl.pallas_call(
        _make_dsa_kernel(n_sel), out_shape=jax.ShapeDtypeStruct((b, s, WIDTH), F32),
        grid=grid, in_specs=in_specs, out_specs=out_specs, scratch_shapes=scratch, name="dsa",
        compiler_params=pltpu.CompilerParams(
            dimension_semantics=("arbitrary", "arbitrary"), vmem_limit_bytes=_vmem_limit(blk)),
    )(qbT, qiT, wiT, ki3, kb3, vbT4, gbias, adm)


def _merge_kernel(x_ref, ya_ref, yb_ref, gate_ref, wa_ref, wb_ref, wo_ref, fg_ref, o_ref):
    d = x_ref.shape[1]
    za = gate_ref[:, 0:WIDTH].astype(F32)
    zb = gate_ref[:, WIDTH:2 * WIDTH].astype(F32)
    ga = gate_ref[:, 2 * WIDTH:2 * WIDTH + d].astype(F32)
    gb = gate_ref[:, 2 * WIDTH + d:2 * WIDTH + 2 * d].astype(F32)
    ua = (ya_ref[...] * (za * _sigmoid(za))).astype(BF16)
    ub = (yb_ref[...] * (zb * _sigmoid(zb))).astype(BF16)
    pa = jnp.dot(ua, wa_ref[...], preferred_element_type=F32)
    pb = jnp.dot(ub, wb_ref[...], preferred_element_type=F32)
    merged = _sigmoid(ga) * pa + _sigmoid(gb) * pb
    h = x_ref[...] + jnp.dot(merged.astype(BF16), wo_ref[...], preferred_element_type=F32)
    ms = jnp.mean(h * h, axis=-1, keepdims=True)
    o_ref[...] = (h * lax.rsqrt(ms + EPS)) * fg_ref[...]


def _merge(x2, ya2, yb2, gates, wa, wb, wo, fg, tm):
    n, d = x2.shape
    n_gate = gates.shape[1]
    grid = (n // tm,)
    row = lambda i: (i, 0)
    const = lambda i: (0, 0)
    in_specs = [
        pl.BlockSpec((tm, d), row),
        pl.BlockSpec((tm, WIDTH), row),
        pl.BlockSpec((tm, WIDTH), row),
        pl.BlockSpec((tm, n_gate), row),
        pl.BlockSpec(wa.shape, const),
        pl.BlockSpec(wb.shape, const),
        pl.BlockSpec(wo.shape, const),
        pl.BlockSpec((1, d), const),
    ]
    blk = 2 * (2 * _nbytes((tm, d), F32) + 2 * _nbytes((tm, WIDTH), F32)
               + _nbytes((tm, n_gate), BF16) + _nbytes(wa.shape, BF16) + _nbytes(wb.shape, BF16)
               + _nbytes(wo.shape, BF16))
    return pl.pallas_call(
        _merge_kernel, out_shape=jax.ShapeDtypeStruct((n, d), F32),
        grid=grid, in_specs=in_specs, out_specs=pl.BlockSpec((tm, d), row), name="merge",
        compiler_params=pltpu.CompilerParams(
            dimension_semantics=("arbitrary",), vmem_limit_bytes=_vmem_limit(blk)),
    )(x2, ya2, yb2, gates, wa, wb, wo, fg)


def _toeplitz(g, nrows, ncols):
    period = nrows + ncols
    lead = g.shape[:-1]
    p = jnp.concatenate([g[..., ::-1], jnp.zeros(lead + (1,), g.dtype)], axis=-1)
    flat = jnp.tile(p, (1,) * len(lead) + (nrows,))[..., :nrows * (period - 1)]
    x = flat.reshape(lead + (nrows, period - 1))
    return x[..., nrows - 1:nrows - 1 + ncols]


def _band_bias(a_rel_bias):
    pad = A_LEFT_CHUNKS * CHUNK
    rel = np.arange(BAND_TQ + BAND_WIN - 1) - (BAND_WIN - 1) + pad
    idx = np.clip(rel, -A_REL_CLIP, A_REL_CLIP) + A_REL_CLIP
    lo, hi = int(idx[0]), int(idx.max())
    n_flat = int((idx == hi).sum()) - 1
    assert np.array_equal(idx, np.minimum(np.arange(lo, lo + idx.size), hi))
    ab = a_rel_bias.astype(F32)
    g = jnp.concatenate([ab[:, lo:hi + 1], jnp.broadcast_to(ab[:, hi:hi + 1], (HEADS, n_flat))],
                        axis=1)
    bias = _toeplitz(g, BAND_TQ, BAND_WIN)
    qq = np.arange(BAND_TQ)[:, None]
    jj = np.arange(BAND_WIN)[None, :]
    in_band = np.logical_and(jj // CHUNK >= qq // CHUNK, jj // CHUNK <= qq // CHUNK + A_LEFT_CHUNKS)
    bias = jnp.where(jnp.asarray(in_band)[None], bias * LOG2E, NEG)
    return jnp.swapaxes(bias, 1, 2)


def _dsa_bias(t5_bias):
    T = DSA_T
    far = T5_BUCKETS // 2 - 1
    tb = t5_bias.astype(F32)
    slabs = [jnp.zeros((HEADS, T, T), F32)]
    for off in (-T, 0):
        rel = jnp.arange(2 * T - 1, dtype=jnp.int32) - (T - 1) + off
        g = (tb[_t5_bucket(rel)] - tb[far]) * LOG2E
        slabs.append(_toeplitz(g.T, T, T))
    return jnp.stack(slabs)


def _far_bucket_is_constant(s):
    half = T5_BUCKETS // 2
    max_exact = half // 2
    for n in (DSA_T + 1, max(s - 1, DSA_T + 1)):
        large = max_exact + int(math.log(n / max_exact) / math.log(T5_MAX_DIST / max_exact)
                                * (half - max_exact))
        if min(large, half - 1) != half - 1:
            return False
    return True


def kernel(x, norm_gain, w_in, a_rel_bias, t5_bias, w_a_out, w_b_out, w_out, final_gain):
    b, s, d = x.shape
    depth = w_in.shape[0]
    n = b * s
    assert s % DSA_T == 0 and s % BAND_TQ == 0
    assert _far_bucket_is_constant(s)
    tm = 512 if n % 512 == 0 else DSA_T
    n_sel = min(TOPK_MAX, s // 4)

    splits = (WIDTH,) * 4 + (WIDTH, HEAD_DIM, HEAD_DIM, WIDTH) + (WIDTH, HEAD_DIM, HEADS) + (d, d)
    offs = np.concatenate([[0], np.cumsum(splits)])
    assert offs[-1] == w_in.shape[2]

    abias = _band_bias
    gbias = _dsa_bias(t5_bias)
    kk = jnp.arange(DSA_T)[:, None] // CHUNK
    qq = jnp.arange(DSA_T)[None, :] // CHUNK
    adm = jnp.stack([jnp.zeros((DSA_T, DSA_T), F32),
                     jnp.where(kk <= qq, 0.0, -jnp.inf).astype(F32),
                     jnp.full((DSA_T, DSA_T), -jnp.inf, F32)])

    h2 = x.reshape(n, d)
    for l in range(depth):
        w = w_in[l]
        (qa, ka, va, za, qb, kb, vb, zb, qi, ki, wi, ga, gb) = [
            w[:, offs[i]:offs[i + 1]] for i in range(len(splits))]
        wn = jnp.concatenate([ka, ki, kb], axis=1).astype(BF16)
        wg = jnp.concatenate([za, zb, ga, gb], axis=1).astype(BF16)
        qscale = HEAD_DIM ** -0.5 * LOG2E
        wt = jnp.concatenate([qa * qscale, va, qb * qscale, qi, vb, wi,
                              jnp.zeros((d, HEADS), w.dtype)], axis=1).T.astype(BF16)

        ka_o, ki_o, kb_o, gates, qaT, vaT, qbT, qiT, vbT, wiT = _proj(
            h2, norm_gain[l].reshape(1, d).astype(F32), wn, wg, wt, tm)

        ya = _band(qaT, ka_o.reshape(b, s, WIDTH),
                   vaT.reshape(b, s // BAND_KB, WIDTH, BAND_KB), abias(a_rel_bias[l]))
        yb = _dsa(qbT, qiT, wiT, ki_o.reshape(b, s, HEAD_DIM), kb_o.reshape(b, s, HEAD_DIM),
                  vbT.reshape(b, s // DSA_T, HEAD_DIM, DSA_T), gbias, adm, n_sel)

        assert depth == 1
        h2 = _merge(h2, ya.reshape(n, WIDTH), yb.reshape(n, WIDTH), gates,
                    w_a_out[l].astype(BF16), w_b_out[l].astype(BF16), w_out[l].astype(BF16),
                    final_gain.reshape(1, d).astype(F32), tm)
    return h2.reshape(b, s, d)
```

```python
import math

import numpy as np
import jax
import jax.numpy as jnp
from jax import lax
from jax.experimental import pallas as pl
from jax.experimental.pallas import tpu as pltpu

F32 = jnp.float32
BF16 = jnp.bfloat16

CHUNK = 64
EPS = 1e-6
HEADS = 8
HEAD_DIM = 64
WIDTH = HEADS * HEAD_DIM
A_LEFT_CHUNKS = 8
A_REL_CLIP = 256
TOPK_MAX = 256
T5_BUCKETS = 32
T5_MAX_DIST = 128
NEG = -1e30
F32_MIN = float(np.finfo(np.float32).min)
LOG2E = math.log2(math.e)

V7X_LANES = 128
V7X_SUBLANES = 8
VMEM_CAP_BYTES = 60000 * 1024
COMPILER_TEMP_BYTES = 8 * 1024 * 1024

BAND_TQ = 4 * CHUNK
BAND_WIN = A_LEFT_CHUNKS * CHUNK + BAND_TQ
BAND_KB = V7X_LANES
BAND_NBLK = BAND_WIN // BAND_KB
DSA_T = 256
VX_ROWS = HEAD_DIM + 2 * V7X_SUBLANES
PV_LAG = 2
BISECT_MAX_ITERS = 26
BISECT_BLIND_ITERS = 16


def _vmem_limit(block_bytes):
    return int(min(block_bytes + COMPILER_TEMP_BYTES, VMEM_CAP_BYTES))


def _nbytes(shape, dtype):
    return int(np.prod(shape)) * jnp.dtype(dtype).itemsize


def _t5_bucket(rel):
    half = T5_BUCKETS // 2
    max_exact = half // 2
    ret = jnp.where(rel > 0, half, 0)
    n = jnp.abs(rel)
    nf = jnp.maximum(n, 1).astype(F32)
    large = max_exact + (jnp.log(nf / max_exact) / math.log(T5_MAX_DIST / max_exact)
                         * (half - max_exact)).astype(jnp.int32)
    large = jnp.minimum(large, half - 1)
    return ret + jnp.where(n < max_exact, n, large)


def _sigmoid(v):
    return 1.0 / (1.0 + jnp.exp(-v))


def _proj_kernel(x_ref, g_ref, wn_ref, wg_ref, wt_ref,
                 ka_ref, ki_ref, kb_ref, gate_ref,
                 qaT_ref, vaT_ref, qbT_ref, qiT_ref, vbT_ref, wiT_ref):
    xf = x_ref[...]
    ms = jnp.mean(xf * xf, axis=-1, keepdims=True)
    hn = ((xf * lax.rsqrt(ms + EPS)) * g_ref[...]).astype(BF16)
    tm = hn.shape[0]

    ka_ref[...] = jnp.dot(hn, wn_ref[:, 0:WIDTH], preferred_element_type=F32).astype(BF16)
    kk = jnp.dot(hn, wn_ref[:, WIDTH:WIDTH + 2 * HEAD_DIM], preferred_element_type=F32)
    ki_ref[...] = kk[:, :HEAD_DIM].astype(BF16)
    kb_ref[...] = kk[:, HEAD_DIM:].astype(BF16)
    n_gate = gate_ref.shape[1]
    for c0 in range(0, n_gate, WIDTH):
        gate_ref[:, c0:c0 + WIDTH] = jnp.dot(
            hn, wg_ref[:, c0:c0 + WIDTH], preferred_element_type=F32).astype(BF16)

    def t_rows(r0, nrows):
        return lax.dot_general(wt_ref[r0:r0 + nrows, :], hn, (((1,), (1,)), ((), ())),
                               preferred_element_type=F32)
    qaT_ref[...] = t_rows(0, WIDTH).astype(BF16)
    vaT = t_rows(WIDTH, WIDTH).astype(BF16)
    for s in range(tm // BAND_KB):
        vaT_ref[s] = vaT[:, s * BAND_KB:(s + 1) * BAND_KB]
    qbT_ref[...] = t_rows(2 * WIDTH, WIDTH).astype(BF16)
    qiT_ref[...] = t_rows(3 * WIDTH, WIDTH).astype(BF16)
    tail = t_rows(4 * WIDTH, HEAD_DIM + 2 * HEADS)
    for s in range(tm // DSA_T):
        vbT_ref[s] = tail[0:HEAD_DIM, s * DSA_T:(s + 1) * DSA_T].astype(BF16)
    wiT_ref[...] = tail[HEAD_DIM:HEAD_DIM + HEADS, :]


def _proj(x2, gain, wn, wg, wt, tm):
    n, d = x2.shape
    n_gate = wg.shape[1]
    grid = (n // tm,)
    const = lambda i: (0, 0)
    row = lambda i: (i, 0)
    col = lambda i: (0, i)
    in_specs = [
        pl.BlockSpec((tm, d), row),
        pl.BlockSpec((1, d), const),
        pl.BlockSpec(wn.shape, const),
        pl.BlockSpec(wg.shape, const),
        pl.BlockSpec(wt.shape, const),
    ]
    out_shape = (
        jax.ShapeDtypeStruct((n, WIDTH), BF16),
        jax.ShapeDtypeStruct((n, HEAD_DIM), BF16),
        jax.ShapeDtypeStruct((n, HEAD_DIM), BF16),
        jax.ShapeDtypeStruct((n, n_gate), BF16),
        jax.ShapeDtypeStruct((WIDTH, n), BF16),
        jax.ShapeDtypeStruct((n // BAND_KB, WIDTH, BAND_KB), BF16),
        jax.ShapeDtypeStruct((WIDTH, n), BF16),
        jax.ShapeDtypeStruct((WIDTH, n), BF16),
        jax.ShapeDtypeStruct((n // DSA_T, HEAD_DIM, DSA_T), BF16),
        jax.ShapeDtypeStruct((HEADS, n), F32),
    )
    out_specs = (
        pl.BlockSpec((tm, WIDTH), row),
        pl.BlockSpec((tm, HEAD_DIM), row),
        pl.BlockSpec((tm, HEAD_DIM), row),
        pl.BlockSpec((tm, n_gate), row),
        pl.BlockSpec((WIDTH, tm), col),
        pl.BlockSpec((tm // BAND_KB, WIDTH, BAND_KB), lambda i: (i, 0, 0)),
        pl.BlockSpec((WIDTH, tm), col),
        pl.BlockSpec((WIDTH, tm), col),
        pl.BlockSpec((tm // DSA_T, HEAD_DIM, DSA_T), lambda i: (i, 0, 0)),
        pl.BlockSpec((HEADS, tm), col),
    )
    blk = 2 * (_nbytes((tm, d), F32) + _nbytes(wn.shape, BF16) + _nbytes(wg.shape, BF16)
               + _nbytes(wt.shape, BF16) + _nbytes((tm, WIDTH), BF16)
               + 2 * _nbytes((tm, V7X_LANES), BF16) + _nbytes((tm, n_gate), BF16)
               + 4 * _nbytes((WIDTH, tm), BF16) + _nbytes((HEAD_DIM, tm), BF16)
               + _nbytes((HEADS, tm), F32))
    return pl.pallas_call(
        _proj_kernel, out_shape=out_shape, grid=grid, in_specs=in_specs, out_specs=out_specs,
        name="proj",
        compiler_params=pltpu.CompilerParams(
            dimension_semantics=("arbitrary",), vmem_limit_bytes=_vmem_limit(blk)),
    )(x2, gain, wn, wg, wt)


def _band_kernel(qT_ref, k_ref, vT_ref, bias_ref, o_ref, s_ref, p_ref):
    t = pl.program_id(1)
    row = lax.broadcasted_iota(jnp.int32, (V7X_LANES, BAND_TQ), 0)
    first_half = row < HEAD_DIM
    n_left = A_LEFT_CHUNKS * CHUNK // BAND_KB
    blocks = []
    for i in range(BAND_NBLK):
        kidx = t * (BAND_TQ // BAND_KB) - n_left + i
        kc = jnp.maximum(kidx, 0)
        blocks.append((kidx, kc, pl.multiple_of(kc * BAND_KB, BAND_KB)))

    def stage_a(h):
        cs = slice(V7X_LANES * (h // 2), V7X_LANES * (h // 2 + 1))
        qT = qT_ref[cs, :]
        qe = jnp.where(first_half if h % 2 == 0 else jnp.logical_not(first_half), qT,
                       jnp.zeros_like(qT))
        m_el = None
        for i, (kidx, kc, ks) in enumerate(blocks):
            ksl = slice(BAND_KB * i, BAND_KB * (i + 1))
            s = jnp.dot(k_ref[pl.ds(ks, BAND_KB), cs], qe, preferred_element_type=F32)
            s = s + bias_ref[h, ksl, :]
            if i < n_left:
                s = jnp.where(kidx >= 0, s, NEG)
            s_ref[h, ksl, :] = s
            m_el = s if m_el is None else jnp.maximum(m_el, s)
        return jnp.max(m_el, axis=0, keepdims=True)

    ones = jnp.ones((VX_ROWS - HEAD_DIM, BAND_KB), BF16)

    def stage_bc(h, m_h):
        p_ref[h] = jnp.exp2(s_ref[h] - m_h).astype(BF16)
        acc = None
        for i, (kidx, kc, ks) in enumerate(blocks):
            vx = jnp.concatenate([vT_ref[kc, HEAD_DIM * h:HEAD_DIM * (h + 1), :], ones], axis=0)
            pv = jnp.dot(vx, p_ref[h, BAND_KB * i:BAND_KB * (i + 1), :],
                         preferred_element_type=F32)
            acc = pv if acc is None else acc + pv
        return acc[0:HEAD_DIM, :] * (1.0 / acc[HEAD_DIM:HEAD_DIM + 1, :])

    m = [stage_a(h) for h in range(HEADS)]
    outs = [stage_bc(h, m[h]) for h in range(HEADS)]
    o_ref[...] = jnp.concatenate(outs, axis=0).T


def _band(qaT, ka3, vaT4, abiasT):
    b, s, _ = ka3.shape
    nt = s // BAND_TQ
    grid = (b, nt)
    once = pl.Buffered(1)
    in_specs = [
        pl.BlockSpec((WIDTH, BAND_TQ), lambda bi, t: (0, bi * nt + t)),
        pl.BlockSpec((None, s, WIDTH), lambda bi, t: (bi, 0, 0), pipeline_mode=once),
        pl.BlockSpec((None, s // BAND_KB, WIDTH, BAND_KB), lambda bi, t: (bi, 0, 0, 0),
                     pipeline_mode=once),
        pl.BlockSpec(abiasT.shape, lambda bi, t: (0, 0, 0), pipeline_mode=once),
    ]
    out_specs = pl.BlockSpec((None, BAND_TQ, WIDTH), lambda bi, t: (bi, t, 0))
    scratch = [
        pltpu.VMEM((HEADS, BAND_WIN, BAND_TQ), F32),
        pltpu.VMEM((HEADS, BAND_WIN, BAND_TQ), BF16),
    ]
    blk = (2 * (_nbytes((WIDTH, BAND_TQ), BF16) + _nbytes((BAND_TQ, WIDTH), F32))
           + 2 * _nbytes((s, WIDTH), BF16) + _nbytes(abiasT.shape, F32)
           + _nbytes((HEADS, BAND_WIN, BAND_TQ), F32) + _nbytes((HEADS, BAND_WIN, BAND_TQ), BF16))
    return pl.pallas_call(
        _band_kernel, out_shape=jax.ShapeDtypeStruct((b, s, WIDTH), F32),
        grid=grid, in_specs=in_specs, out_specs=out_specs, scratch_shapes=scratch, name="band",
        compiler_params=pltpu.CompilerParams(
            dimension_semantics=("arbitrary", "arbitrary"), vmem_limit_bytes=_vmem_limit(blk)),
    )(qaT, ka3, vaT4, abiasT)


def _make_dsa_kernel(n_sel):
    T = DSA_T
    kf = float(n_sel)
    idx_scale = (HEADS ** -0.5) * (HEAD_DIM ** -0.5)

    def kernel(qbT_ref, qiT_ref, wiT_ref, ki_ref, kb_ref, vbT_ref, gbias_ref, adm_ref, o_ref,
               sc_ref, m_ref, acc_ref, tmax_ref, s_ref, p_ref, vx_ref, cz_ref):
        j = pl.program_id(1)
        nt = j + 1
        zrow = jnp.zeros((1, T), F32)

        def tile_rows(kt):
            return pl.ds(pl.multiple_of(kt * T, T), T)

        wis = wiT_ref[...] * idx_scale

        def fold8(a):
            return jnp.sum(a.reshape(a.shape[0] // V7X_SUBLANES, V7X_SUBLANES, T), axis=0)

        HT = T // 2

        raw_ref = s_ref.at[0]

        def p1_half(kt_dot, kt_post, half, carry):
            hs = slice(half * HT, (half + 1) * HT)
            if kt_post is not None:
                mn, mx, cgt0, cge0 = carry
                acc = raw_ref[hs, :]
                mn = jnp.minimum(mn, jnp.min(acc, axis=0, keepdims=True))
                mx = jnp.maximum(mx, jnp.max(acc, axis=0, keepdims=True))
                slab = (kt_post == j).astype(jnp.int32) + 2 * (kt_post > j).astype(jnp.int32)
                sc = acc + adm_ref[slab, hs, :]
                sc_ref[pl.ds(pl.multiple_of(kt_post * T + half * HT, HT), HT), :] = sc
                cgt0 = cgt0 + fold8(jnp.where(sc > 0.0, 1.0, 0.0))
                cge0 = cge0 + fold8(jnp.where(sc >= 0.0, 1.0, 0.0))
                if half == 1:
                    cz_ref[kt_post + 1] = cge0 - cgt0
                carry = (mn, mx, cgt0, cge0)
            if kt_dot is not None:
                kt_ld = jnp.minimum(kt_dot, pl.num_programs(1) - 1)
                ki_t = ki_ref[pl.ds(pl.multiple_of(kt_ld * T + half * HT, HT), HT), :]
                acc = jnp.zeros((HT, T), F32)
                for h in range(HEADS):
                    lg = jnp.dot(ki_t, qiT_ref[HEAD_DIM * h:HEAD_DIM * (h + 1), :],
                                 preferred_element_type=F32)
                    acc = acc + jnp.maximum(lg, 0.0) * wis[h:h + 1, :]
                raw_ref[hs, :] = acc
            return carry

        ntp = nt + nt % 2

        def p1_tile(kt_dot, kt_post, carry):
            for half in range(2):
                carry = p1_half(kt_dot, kt_post, half, carry)
            return carry

        def p1(i, carry):
            carry = p1_tile(2 * i + 1, 2 * i, carry)
            return p1_tile(2 * i + 2, 2 * i + 1, carry)

        z8 = jnp.zeros((V7X_SUBLANES, T), F32)
        cz_ref[0] = z8
        carry = (jnp.full((1, T), jnp.inf, F32), jnp.full((1, T), -jnp.inf, F32), z8, z8)
        p1_tile(0, None, None)
        carry = lax.fori_loop(0, ntp // 2 - 1, p1, carry)
        carry = p1_tile(ntp - 1, ntp - 2, carry)
        carry = p1_tile(None, ntp - 1, carry)
        lo0, hi0, cgt0, cge0 = carry
        cgt0 = jnp.sum(cgt0, axis=0, keepdims=True)
        cge0 = jnp.sum(cge0, axis=0, keepdims=True)

        qpos = j * T + lax.broadcasted_iota(jnp.int32, (1, T), 1)
        n_adm = ((qpos // CHUNK + 1) * CHUNK).astype(F32)
        small = n_adm <= kf
        above0 = cgt0 >= kf
        below0 = cge0 < kf
        zero_tie = jnp.logical_not(jnp.logical_or(above0, below0))
        lo_pos = lo0 > 0.0
        lo1 = jnp.where(zero_tie, 0.0, jnp.where(jnp.logical_and(above0, jnp.logical_not(lo_pos)),
                                                 0.0, lo0))
        cl1 = jnp.where(jnp.logical_or(zero_tie, jnp.logical_and(above0, jnp.logical_not(lo_pos))),
                        cge0, n_adm)
        hi1 = jnp.where(zero_tie, 0.0, jnp.where(below0, jnp.minimum(hi0, 0.0), hi0))

        def count_ge(th):
            def body(i, c8):
                for u in range(2):
                    blk = sc_ref[tile_rows(2 * i + u), :]
                    c8 = c8 + fold8(jnp.where(blk >= th, 1.0, 0.0))
                return c8
            return jnp.sum(lax.fori_loop(0, ntp // 2, body, z8), axis=0, keepdims=True)

        def any_open(done):
            return jnp.max(jnp.where(done, 0.0, 1.0)) > 0.0

        def settled(cl_):
            return jnp.logical_or(jnp.logical_or(small, zero_tie), cl_ == kf)

        def bisect(lo, hi, cl):
            mid = 0.5 * lo + 0.5 * hi
            c = count_ge(mid)
            ge = c >= kf
            return jnp.where(ge, mid, lo), jnp.where(ge, hi, mid), jnp.where(ge, c, cl)

        def b_cond(st):
            it, _, _, _, open_ = st
            return jnp.logical_and(it < BISECT_MAX_ITERS, open_)

        def b_body(st):
            it, lo, hi, cl, _ = st
            lo, hi, cl = bisect(lo, hi, cl)
            return it + 1, lo, hi, cl, any_open(settled(cl))

        open1 = any_open(settled(cl1))
        n_blind = jnp.where(open1, BISECT_BLIND_ITERS, 0)
        lo, hi, cl = lax.fori_loop(0, n_blind, lambda _, st: bisect(*st), (lo1, hi1, cl1))
        _, lo, hi, cl, open_ = lax.while_loop(
            b_cond, b_body, (n_blind, lo, hi, cl, jnp.logical_and(open1, any_open(settled(cl)))))
        has_ties = jnp.logical_or(open_, jnp.max(jnp.where(
            jnp.logical_and(zero_tie, jnp.logical_not(small)), 1.0, 0.0)) > 0.0)

        def fast_mask():
            thr = jnp.where(small, F32_MIN, lo)

            def body(kt, _):
                blk = sc_ref[tile_rows(kt), :]
                sc_ref[tile_rows(kt), :] = jnp.where(blk >= thr, 0.0, NEG)
                return 0
            lax.fori_loop(0, nt, body, 0)

        def exact_mask():
            def done_of(lo_, hi_, cl_):
                return jnp.logical_or(jnp.logical_or(small, cl_ == kf), lo_ >= hi_)

            def s_body(st):
                lo_, hi_, cl_, _ = st
                done = done_of(lo_, hi_, cl_)
                mid = 0.5 * lo_ + 0.5 * hi_
                mid = jnp.where(mid > lo_, mid, hi_)

                def body(kt, c3):
                    c, mn_ge, mx_lt = c3
                    blk = sc_ref[tile_rows(kt), :]
                    ge_ = blk >= mid
                    c = c + jnp.sum(jnp.where(ge_, 1.0, 0.0), axis=0, keepdims=True)
                    mn_ge = jnp.minimum(mn_ge, jnp.min(jnp.where(ge_, blk, jnp.inf),
                                                       axis=0, keepdims=True))
                    mx_lt = jnp.maximum(mx_lt, jnp.max(jnp.where(ge_, -jnp.inf, blk),
                                                       axis=0, keepdims=True))
                    return c, mn_ge, mx_lt
                c, mn_ge, mx_lt = lax.fori_loop(
                    0, nt, body, (zrow, jnp.full((1, T), jnp.inf, F32),
                                  jnp.full((1, T), -jnp.inf, F32)))
                ge = c >= kf
                lo_n = jnp.where(done, lo_, jnp.where(ge, mn_ge, lo_))
                cl_n = jnp.where(done, cl_, jnp.where(ge, c, cl_))
                hi_n = jnp.where(done, hi_, jnp.where(ge, hi_, mx_lt))
                return lo_n, hi_n, cl_n, any_open(done_of(lo_n, hi_n, cl_n))

            lo_e, _, cl_e, _ = lax.while_loop(
                lambda st: st[3], s_body, (lo, hi, cl, any_open(done_of(lo, hi, cl))))
            thr = jnp.where(small, F32_MIN, lo_e)

            def count_gt():
                def body(kt, c8):
                    blk = sc_ref[tile_rows(kt), :]
                    return c8 + fold8(jnp.where(blk > thr, 1.0, 0.0))
                return jnp.sum(lax.fori_loop(0, nt, body, z8), axis=0, keepdims=True)
            cgt = lax.cond(open_, count_gt, lambda: cgt0)
            need = jnp.where(jnp.logical_or(small, cl_e == kf), jnp.inf, kf - cgt)

            UNDECIDED = -1.0

            def m_body(kt, st):
                carry, xt, xc = st
                blk = sc_ref[tile_rows(kt), :]
                eq = jnp.where(blk == thr, 1.0, 0.0)
                cnt = jnp.sum(fold8(eq), axis=0, keepdims=True)
                has_budget = carry < need
                crossing = jnp.logical_and(has_budget, carry + cnt > need)
                tie_val = jnp.where(crossing, UNDECIDED, jnp.where(has_budget, 0.0, NEG))
                sc_ref[tile_rows(kt), :] = jnp.where(
                    blk > thr, 0.0, jnp.where(blk == thr, tie_val, NEG))
                xt = jnp.where(crossing, kt.astype(F32), xt)
                xc = jnp.where(crossing, carry, xc)
                return carry + cnt, xt, xc
            def mz_body(kt, st):
                xt, xc = st
                blk = sc_ref[tile_rows(kt), :]
                before = jnp.sum(cz_ref[kt], axis=0, keepdims=True)
                after = jnp.sum(cz_ref[kt + 1], axis=0, keepdims=True)
                has_budget = before < need
                crossing = jnp.logical_and(has_budget, after > need)
                tie_val = jnp.where(crossing, UNDECIDED, jnp.where(has_budget, 0.0, NEG))
                sc_ref[tile_rows(kt), :] = jnp.where(
                    blk > thr, 0.0, jnp.where(blk == thr, tie_val, NEG))
                return jnp.where(crossing, kt.astype(F32), xt), jnp.where(crossing, before, xc)

            no_x = jnp.full((1, T), -1.0, F32)
            xt, xc = lax.cond(
                open_,
                lambda: lax.fori_loop(0, nt, m_body, (zrow, no_x, zrow))[1:],
                lambda: lax.fori_loop(0, nt, mz_body, (no_x, zrow)))

            def r_body(st):
                xt_, ktf = st
                kt = ktf.astype(jnp.int32)
                msk = sc_ref[tile_rows(kt), :]
                und = msk == UNDECIDED
                r_io = lax.broadcasted_iota(jnp.int32, (T, T), 0)
                c_io = lax.broadcasted_iota(jnp.int32, (T, T), 1)
                ltri = jnp.where(c_io < r_io, 1.0, 0.0).astype(BF16)
                rank = jnp.dot(ltri, jnp.where(und, 1.0, 0.0).astype(BF16),
                               preferred_element_type=F32) + xc
                sc_ref[tile_rows(kt), :] = jnp.where(
                    und, jnp.where(rank < need, 0.0, NEG), msk)
                xt_ = jnp.where(xt_ == ktf, -1.0, xt_)
                return xt_, jnp.max(xt_)
            lax.while_loop(lambda st: st[1] >= 0.0, r_body, (xt, jnp.max(xt)))

        lax.cond(has_ties, exact_mask, fast_mask)

        m_ref[...] = jnp.full(m_ref.shape, NEG, F32)
        acc_ref[...] = jnp.zeros(acc_ref.shape, F32)
        vx_ref[HEAD_DIM:, :] = jnp.ones((VX_ROWS - HEAD_DIM, T), BF16)

        def step(kt_a, bias_slab, kt_bc):
            if kt_bc is not None:
                vx_ref[0:HEAD_DIM, :] = vbT_ref[kt_bc]
                m_old = m_ref[...]
                m_new = jnp.maximum(m_old, tmax_ref[...])
                alpha = jnp.exp2(m_old - m_new)
                m_ref[...] = m_new
            if kt_a is not None:
                kb_t = kb_ref[tile_rows(kt_a), :]
                msk = sc_ref[tile_rows(kt_a), :]
            tmax = []
            for h in range(HEADS):
                rows = slice(HEAD_DIM * h, HEAD_DIM * (h + 1))
                if kt_bc is not None:
                    p_ref[h] = jnp.exp2(s_ref[h] - m_new[h:h + 1, :]).astype(BF16)
                if kt_a is not None:
                    add = msk if bias_slab is None else msk + gbias_ref[bias_slab, h]
                    s = jnp.dot(kb_t, qbT_ref[rows, :], preferred_element_type=F32) + add
                    s_ref[h] = s
                    tmax.append(jnp.max(s, axis=0, keepdims=True))
                if kt_bc is not None:
                    for hc in ([h - PV_LAG] if h < HEADS - 1 else range(h - PV_LAG, HEADS)):
                        if hc < 0:
                            continue
                        acc_ref[hc] = alpha[hc:hc + 1, :] * acc_ref[hc] + jnp.dot(
                            vx_ref[...], p_ref[hc], preferred_element_type=F32)
            if kt_a is not None:
                tmax_ref[...] = jnp.concatenate(tmax, axis=0)

        n_far = jnp.maximum(j - 1, 0)
        step(0, jnp.clip(2 - j, 0, 2), None)

        def far_step(a, _):
            step(a, None, a - 1)
            return 0

        def near_step(a, _):
            step(a, a - j + 2, a - 1)
            return 0

        lax.fori_loop(1, n_far, far_step, 0)
        lax.fori_loop(jnp.maximum(n_far, 1), nt, near_step, 0)
        step(None, None, nt - 1)

        outs = []
        for h in range(HEADS):
            a = acc_ref[h]
            outs.append(a[0:HEAD_DIM, :] * (1.0 / a[HEAD_DIM:HEAD_DIM + 1, :]))
        o_ref[...] = jnp.concatenate(outs, axis=0).T

    return kernel


def _dsa(qbT, qiT, wiT, ki3, kb3, vbT4, gbias, adm, n_sel):
    b, s, _ = ki3.shape
    T = DSA_T
    nq = s // T
    grid = (b, nq)
    in_specs = [
        pl.BlockSpec((WIDTH, T), lambda bi, j: (0, bi * nq + j)),
        pl.BlockSpec((WIDTH, T), lambda bi, j: (0, bi * nq + j)),
        pl.BlockSpec((HEADS, T), lambda bi, j: (0, bi * nq + j)),
        pl.BlockSpec((None, s, HEAD_DIM), lambda bi, j: (bi, 0, 0)),
        pl.BlockSpec((None, s, HEAD_DIM), lambda bi, j: (bi, 0, 0)),
        pl.BlockSpec((None, nq, HEAD_DIM, T), lambda bi, j: (bi, 0, 0, 0)),
        pl.BlockSpec(gbias.shape, lambda bi, j: (0, 0, 0, 0)),
        pl.BlockSpec(adm.shape, lambda bi, j: (0, 0, 0)),
    ]
    out_specs = pl.BlockSpec((None, T, WIDTH), lambda bi, j: (bi, j, 0))
    scratch = [
        pltpu.VMEM((s + T, T), F32),
        pltpu.VMEM((HEADS, T), F32),
        pltpu.VMEM((HEADS, VX_ROWS, T), F32),
        pltpu.VMEM((HEADS, T), F32),
        pltpu.VMEM((HEADS, T, T), F32),
        pltpu.VMEM((HEADS, T, T), BF16),
        pltpu.VMEM((VX_ROWS, T), BF16),
        pltpu.VMEM((nq + 2, V7X_SUBLANES, T), F32),
    ]
    blk = (2 * (2 * _nbytes((WIDTH, T), BF16) + _nbytes((HEADS, T), F32)
                + 2 * _nbytes((s, V7X_LANES), BF16) + _nbytes((HEAD_DIM, s), BF16)
                + _nbytes(gbias.shape, F32) + _nbytes(adm.shape, F32)
                + _nbytes((T, WIDTH), F32))
           + _nbytes((s + T, T), F32) + 2 * _nbytes((HEADS, T), F32)
           + _nbytes((HEADS, VX_ROWS, T), F32) + _nbytes((HEADS, T, T), F32)
           + _nbytes((HEADS, T, T), BF16) + _nbytes((VX_ROWS, T), BF16)
           + _nbytes((nq + 2, V7X_SUBLANES, T), F32))
    return pl.pallas_call(
        _make_dsa_kernel(n_sel), out_shape=jax.ShapeDtypeStruct((b, s, WIDTH), F32),
        grid=grid, in_specs=in_specs, out_specs=out_specs, scratch_shapes=scratch, name="dsa",
        compiler_params=pltpu.CompilerParams(
            dimension_semantics=("arbitrary", "arbitrary"), vmem_limit_bytes=_vmem_limit(blk)),
    )(qbT, qiT, wiT, ki3, kb3, vbT4, gbias, adm)


def _merge_kernel(x_ref, ya_ref, yb_ref, gate_ref, wa_ref, wb_ref, wo_ref, fg_ref, o_ref):
    d = x_ref.shape[1]
    za = gate_ref[:, 0:WIDTH].astype(F32)
    zb = gate_ref[:, WIDTH:2 * WIDTH].astype(F32)
    ga = gate_ref[:, 2 * WIDTH:2 * WIDTH + d].astype(F32)
    gb = gate_ref[:, 2 * WIDTH + d:2 * WIDTH + 2 * d].astype(F32)
    ua = (ya_ref[...] * (za * _sigmoid(za))).astype(BF16)
    ub = (yb_ref[...] * (zb * _sigmoid(zb))).astype(BF16)
    pa = jnp.dot(ua, wa_ref[...], preferred_element_type=F32)
    pb = jnp.dot(ub, wb_ref[...], preferred_element_type=F32)
    merged = _sigmoid(ga) * pa + _sigmoid(gb) * pb
    h = x_ref[...] + jnp.dot(merged.astype(BF16), wo_ref[...], preferred_element_type=F32)
    ms = jnp.mean(h * h, axis=-1, keepdims=True)
    o_ref[...] = (h * lax.rsqrt(ms + EPS)) * fg_ref[...]


def _merge(x2, ya2, yb2, gates, wa, wb, wo, fg, tm):
    n, d = x2.shape
    n_gate = gates.shape[1]
    grid = (n // tm,)
    row = lambda i: (i, 0)
    const = lambda i: (0, 0)
    in_specs = [
        pl.BlockSpec((tm, d), row),
        pl.BlockSpec((tm, WIDTH), row),
        pl.BlockSpec((tm, WIDTH), row),
        pl.BlockSpec((tm, n_gate), row),
        pl.BlockSpec(wa.shape, const),
        pl.BlockSpec(wb.shape, const),
        pl.BlockSpec(wo.shape, const),
        pl.BlockSpec((1, d), const),
    ]
    blk = 2 * (2 * _nbytes((tm, d), F32) + 2 * _nbytes((tm, WIDTH), F32)
               + _nbytes((tm, n_gate), BF16) + _nbytes(wa.shape, BF16) + _nbytes(wb.shape, BF16)
               + _nbytes(wo.shape, BF16))
    return pl.pallas_call(
        _merge_kernel, out_shape=jax.ShapeDtypeStruct((n, d), F32),
        grid=grid, in_specs=in_specs, out_specs=pl.BlockSpec((tm, d), row), name="merge",
        compiler_params=pltpu.CompilerParams(
            dimension_semantics=("arbitrary",), vmem_limit_bytes=_vmem_limit(blk)),
    )(x2, ya2, yb2, gates, wa, wb, wo, fg)


def _toeplitz(g, nrows, ncols):
    period = nrows + ncols
    lead = g.shape[:-1]
    p = jnp.concatenate([g[..., ::-1], jnp.zeros(lead + (1,), g.dtype)], axis=-1)
    flat = jnp.tile(p, (1,) * len(lead) + (nrows,))[..., :nrows * (period - 1)]
    x = flat.reshape(lead + (nrows, period - 1))
    return x[..., nrows - 1:nrows - 1 + ncols]


def _band_bias(a_rel_bias):
    pad = A_LEFT_CHUNKS * CHUNK
    rel = np.arange(BAND_TQ + BAND_WIN - 1) - (BAND_WIN - 1) + pad
    idx = np.clip(rel, -A_REL_CLIP, A_REL_CLIP) + A_REL_CLIP
    lo, hi = int(idx[0]), int(idx.max())
    n_flat = int((idx == hi).sum()) - 1
    assert np.array_equal(idx, np.minimum(np.arange(lo, lo + idx.size), hi))
    ab = a_rel_bias.astype(F32)
    g = jnp.concatenate([ab[:, lo:hi + 1], jnp.broadcast_to(ab[:, hi:hi + 1], (HEADS, n_flat))],
                        axis=1)
    bias = _toeplitz(g, BAND_TQ, BAND_WIN)
    qq = np.arange(BAND_TQ)[:, None]
    jj = np.arange(BAND_WIN)[None, :]
    in_band = np.logical_and(jj // CHUNK >= qq // CHUNK, jj // CHUNK <= qq // CHUNK + A_LEFT_CHUNKS)
    bias = jnp.where(jnp.asarray(in_band)[None], bias * LOG2E, NEG)
    return jnp.swapaxes(bias, 1, 2)


def _dsa_bias(t5_bias):
    T = DSA_T
    far = T5_BUCKETS // 2 - 1
    tb = t5_bias.astype(F32)
    slabs = [jnp.zeros((HEADS, T, T), F32)]
    for off in (-T, 0):
        rel = jnp.arange(2 * T - 1, dtype=jnp.int32) - (T - 1) + off
        g = (tb[_t5_bucket(rel)] - tb[far]) * LOG2E
        slabs.append(_toeplitz(g.T, T, T))
    return jnp.stack(slabs)


def _far_bucket_is_constant(s):
    half = T5_BUCKETS // 2
    max_exact = half // 2
    for n in (DSA_T + 1, max(s - 1, DSA_T + 1)):
        large = max_exact + int(math.log(n / max_exact) / math.log(T5_MAX_DIST / max_exact)
                                * (half - max_exact))
        if min(large, half - 1) != half - 1:
            return False
    return True


def kernel(x, norm_gain, w_in, a_rel_bias, t5_bias, w_a_out, w_b_out, w_out, final_gain):
    b, s, d = x.shape
    depth = w_in.shape[0]
    n = b * s
    assert s % DSA_T == 0 and s % BAND_TQ == 0
    assert _far_bucket_is_constant(s)
    tm = 512 if n % 512 == 0 else DSA_T
    n_sel = min(TOPK_MAX, s // 4)

    splits = (WIDTH,) * 4 + (WIDTH, HEAD_DIM, HEAD_DIM, WIDTH) + (WIDTH, HEAD_DIM, HEADS) + (d, d)
    offs = np.concatenate([[0], np.cumsum(splits)])
    assert offs[-1] == w_in.shape[2]

    abias = _band_bias
    gbias = _dsa_bias(t5_bias)
    kk = jnp.arange(DSA_T)[:, None] // CHUNK
    qq = jnp.arange(DSA_T)[None, :] // CHUNK
    adm = jnp.stack([jnp.zeros((DSA_T, DSA_T), F32),
                     jnp.where(kk <= qq, 0.0, -jnp.inf).astype(F32),
                     jnp.full((DSA_T, DSA_T), -jnp.inf, F32)])

    h2 = x.reshape(n, d)
    for l in range(depth):
        w = w_in[l]
        (qa, ka, va, za, qb, kb, vb, zb, qi, ki, wi, ga, gb) = [
            w[:, offs[i]:offs[i + 1]] for i in range(len(splits))]
        wn = jnp.concatenate([ka, ki, kb], axis=1).astype(BF16)
        wg = jnp.concatenate([za, zb, ga, gb], axis=1).astype(BF16)
        qscale = HEAD_DIM ** -0.5 * LOG2E
        wt = jnp.concatenate([qa * qscale, va, qb * qscale, qi, vb, wi,
                              jnp.zeros((d, HEADS), w.dtype)], axis=1).T.astype(BF16)

        ka_o, ki_o, kb_o, gates, qaT, vaT, qbT, qiT, vbT, wiT = _proj(
            h2, norm_gain[l].reshape(1, d).astype(F32), wn, wg, wt, tm)

        ya = _band(qaT, ka_o.reshape(b, s, WIDTH),
                   vaT.reshape(b, s // BAND_KB, WIDTH, BAND_KB), abias(a_rel_bias[l]))
        yb = _dsa(qbT, qiT, wiT, ki_o.reshape(b, s, HEAD_DIM), kb_o.reshape(b, s, HEAD_DIM),
                  vbT.reshape(b, s // DSA_T, HEAD_DIM, DSA_T), gbias, adm, n_sel)

        assert depth == 1
        h2 = _merge(h2, ya.reshape(n, WIDTH), yb.reshape(n, WIDTH), gates,
                    w_a_out[l].astype(BF16), w_b_out[l].astype(BF16), w_out[l].astype(BF16),
                    final_gain.reshape(1, d).astype(F32), tm)
    return h2.reshape(b, s, d)
```

```python
import math

import numpy as np
import jax
import jax.numpy as jnp
from jax import lax
from jax.experimental import pallas as pl
from jax.experimental.pallas import tpu as pltpu

F32 = jnp.float32
BF16 = jnp.bfloat16

CHUNK = 64
EPS = 1e-6
HEADS = 8
HEAD_DIM = 64
WIDTH = HEADS * HEAD_DIM
A_LEFT_CHUNKS = 8
A_REL_CLIP = 256
TOPK_MAX = 256
T5_BUCKETS = 32
T5_MAX_DIST = 128
NEG = -1e30
F32_MIN = float(np.finfo(np.float32).min)
LOG2E = math.log2(math.e)

V7X_LANES = 128
V7X_SUBLANES = 8
VMEM_CAP_BYTES = 60000 * 1024
COMPILER_TEMP_BYTES = 8 * 1024 * 1024

BAND_TQ = 4 * CHUNK
BAND_WIN = A_LEFT_CHUNKS * CHUNK + BAND_TQ
BAND_KB = V7X_LANES
BAND_NBLK = BAND_WIN // BAND_KB
DSA_T = 256
VX_ROWS = HEAD_DIM + 2 * V7X_SUBLANES
PV_LAG = 2
MERGE_COLS = 512
BISECT_MAX_ITERS = 26
BISECT_BLIND_ITERS = 16


def _vmem_limit(block_bytes):
    return int(min(block_bytes + COMPILER_TEMP_BYTES, VMEM_CAP_BYTES))


def _nbytes(shape, dtype):
    return int(np.prod(shape)) * jnp.dtype(dtype).itemsize


def _t5_bucket(rel):
    half = T5_BUCKETS // 2
    max_exact = half // 2
    ret = jnp.where(rel > 0, half, 0)
    n = jnp.abs(rel)
    nf = jnp.maximum(n, 1).astype(F32)
    large = max_exact + (jnp.log(nf / max_exact) / math.log(T5_MAX_DIST / max_exact)
                         * (half - max_exact)).astype(jnp.int32)
    large = jnp.minimum(large, half - 1)
    return ret + jnp.where(n < max_exact, n, large)


def _sigmoid(v):
    return 1.0 / (1.0 + jnp.exp(-v))


def _proj_kernel(x_ref, g_ref, wn_ref, wg_ref, wt_ref,
                 ka_ref, ki_ref, kb_ref, gate_ref,
                 qaT_ref, vaT_ref, qbT_ref, qiT_ref, vbT_ref, wiT_ref):
    xf = x_ref[...]
    ms = jnp.mean(xf * xf, axis=-1, keepdims=True)
    hn = ((xf * lax.rsqrt(ms + EPS)) * g_ref[...]).astype(BF16)
    tm = hn.shape[0]

    ka_ref[...] = jnp.dot(hn, wn_ref[:, 0:WIDTH], preferred_element_type=F32).astype(BF16)
    kk = jnp.dot(hn, wn_ref[:, WIDTH:WIDTH + 2 * HEAD_DIM], preferred_element_type=F32)
    ki_ref[...] = kk[:, :HEAD_DIM].astype(BF16)
    kb_ref[...] = kk[:, HEAD_DIM:].astype(BF16)
    n_gate = gate_ref.shape[1]
    for c0 in range(0, n_gate, WIDTH):
        gate_ref[:, c0:c0 + WIDTH] = jnp.dot(
            hn, wg_ref[:, c0:c0 + WIDTH], preferred_element_type=F32).astype(BF16)

    def t_rows(r0, nrows):
        return lax.dot_general(wt_ref[r0:r0 + nrows, :], hn, (((1,), (1,)), ((), ())),
                               preferred_element_type=F32)
    qaT_ref[...] = t_rows(0, WIDTH).astype(BF16)
    vaT = t_rows(WIDTH, WIDTH).astype(BF16)
    for s in range(tm // BAND_KB):
        vaT_ref[s] = vaT[:, s * BAND_KB:(s + 1) * BAND_KB]
    qbT_ref[...] = t_rows(2 * WIDTH, WIDTH).astype(BF16)
    qiT_ref[...] = t_rows(3 * WIDTH, WIDTH).astype(BF16)
    tail = t_rows(4 * WIDTH, HEAD_DIM + 2 * HEADS)
    for s in range(tm // DSA_T):
        vbT_ref[s] = tail[0:HEAD_DIM, s * DSA_T:(s + 1) * DSA_T].astype(BF16)
    wiT_ref[...] = tail[HEAD_DIM:HEAD_DIM + HEADS, :]


def _proj(x2, gain, wn, wg, wt, tm):
    n, d = x2.shape
    n_gate = wg.shape[1]
    grid = (n // tm,)
    const = lambda i: (0, 0)
    row = lambda i: (i, 0)
    col = lambda i: (0, i)
    in_specs = [
        pl.BlockSpec((tm, d), row),
        pl.BlockSpec((1, d), const),
        pl.BlockSpec(wn.shape, const),
        pl.BlockSpec(wg.shape, const),
        pl.BlockSpec(wt.shape, const),
    ]
    out_shape = (
        jax.ShapeDtypeStruct((n, WIDTH), BF16),
        jax.ShapeDtypeStruct((n, HEAD_DIM), BF16),
        jax.ShapeDtypeStruct((n, HEAD_DIM), BF16),
        jax.ShapeDtypeStruct((n, n_gate), BF16),
        jax.ShapeDtypeStruct((WIDTH, n), BF16),
        jax.ShapeDtypeStruct((n // BAND_KB, WIDTH, BAND_KB), BF16),
        jax.ShapeDtypeStruct((WIDTH, n), BF16),
        jax.ShapeDtypeStruct((WIDTH, n), BF16),
        jax.ShapeDtypeStruct((n // DSA_T, HEAD_DIM, DSA_T), BF16),
        jax.ShapeDtypeStruct((HEADS, n), F32),
    )
    out_specs = (
        pl.BlockSpec((tm, WIDTH), row),
        pl.BlockSpec((tm, HEAD_DIM), row),
        pl.BlockSpec((tm, HEAD_DIM), row),
        pl.BlockSpec((tm, n_gate), row),
        pl.BlockSpec((WIDTH, tm), col),
        pl.BlockSpec((tm // BAND_KB, WIDTH, BAND_KB), lambda i: (i, 0, 0)),
        pl.BlockSpec((WIDTH, tm), col),
        pl.BlockSpec((WIDTH, tm), col),
        pl.BlockSpec((tm // DSA_T, HEAD_DIM, DSA_T), lambda i: (i, 0, 0)),
        pl.BlockSpec((HEADS, tm), col),
    )
    blk = 2 * (_nbytes((tm, d), F32) + _nbytes(wn.shape, BF16) + _nbytes(wg.shape, BF16)
               + _nbytes(wt.shape, BF16) + _nbytes((tm, WIDTH), BF16)
               + 2 * _nbytes((tm, V7X_LANES), BF16) + _nbytes((tm, n_gate), BF16)
               + 4 * _nbytes((WIDTH, tm), BF16) + _nbytes((HEAD_DIM, tm), BF16)
               + _nbytes((HEADS, tm), F32))
    return pl.pallas_call(
        _proj_kernel, out_shape=out_shape, grid=grid, in_specs=in_specs, out_specs=out_specs,
        name="proj",
        compiler_params=pltpu.CompilerParams(
            dimension_semantics=("arbitrary",), vmem_limit_bytes=_vmem_limit(blk)),
    )(x2, gain, wn, wg, wt)


def _band_kernel(qT_ref, k_ref, vT_ref, bias_ref, o_ref, s_ref, p_ref):
    t = pl.program_id(1)
    row = lax.broadcasted_iota(jnp.int32, (V7X_LANES, BAND_TQ), 0)
    first_half = row < HEAD_DIM
    n_left = A_LEFT_CHUNKS * CHUNK // BAND_KB
    blocks = []
    for i in range(BAND_NBLK):
        kidx = t * (BAND_TQ // BAND_KB) - n_left + i
        kc = jnp.maximum(kidx, 0)
        blocks.append((kidx, kc, pl.multiple_of(kc * BAND_KB, BAND_KB)))

    def stage_a(h):
        cs = slice(V7X_LANES * (h // 2), V7X_LANES * (h // 2 + 1))
        qT = qT_ref[cs, :]
        qe = jnp.where(first_half if h % 2 == 0 else jnp.logical_not(first_half), qT,
                       jnp.zeros_like(qT))
        m_el = None
        for i, (kidx, kc, ks) in enumerate(blocks):
            ksl = slice(BAND_KB * i, BAND_KB * (i + 1))
            s = jnp.dot(k_ref[pl.ds(ks, BAND_KB), cs], qe, preferred_element_type=F32)
            s = s + bias_ref[h, ksl, :]
            if i < n_left:
                s = jnp.where(kidx >= 0, s, NEG)
            s_ref[h, ksl, :] = s
            m_el = s if m_el is None else jnp.maximum(m_el, s)
        return jnp.max(m_el, axis=0, keepdims=True)

    ones = jnp.ones((VX_ROWS - HEAD_DIM, BAND_KB), BF16)

    def stage_bc(h, m_h):
        p_ref[h] = jnp.exp2(s_ref[h] - m_h).astype(BF16)
        acc = None
        for i, (kidx, kc, ks) in enumerate(blocks):
            vx = jnp.concatenate([vT_ref[kc, HEAD_DIM * h:HEAD_DIM * (h + 1), :], ones], axis=0)
            pv = jnp.dot(vx, p_ref[h, BAND_KB * i:BAND_KB * (i + 1), :],
                         preferred_element_type=F32)
            acc = pv if acc is None else acc + pv
        return acc[0:HEAD_DIM, :] * (1.0 / acc[HEAD_DIM:HEAD_DIM + 1, :])

    m = [stage_a(h) for h in range(HEADS)]
    outs = [stage_bc(h, m[h]) for h in range(HEADS)]
    o_ref[...] = jnp.concatenate(outs, axis=0).T


def _band(qaT, ka3, vaT4, abiasT):
    b, s, _ = ka3.shape
    nt = s // BAND_TQ
    grid = (b, nt)
    once = pl.Buffered(1)
    in_specs = [
        pl.BlockSpec((WIDTH, BAND_TQ), lambda bi, t: (0, bi * nt + t)),
        pl.BlockSpec((None, s, WIDTH), lambda bi, t: (bi, 0, 0), pipeline_mode=once),
        pl.BlockSpec((None, s // BAND_KB, WIDTH, BAND_KB), lambda bi, t: (bi, 0, 0, 0),
                     pipeline_mode=once),
        pl.BlockSpec(abiasT.shape, lambda bi, t: (0, 0, 0), pipeline_mode=once),
    ]
    out_specs = pl.BlockSpec((None, BAND_TQ, WIDTH), lambda bi, t: (bi, t, 0))
    scratch = [
        pltpu.VMEM((HEADS, BAND_WIN, BAND_TQ), F32),
        pltpu.VMEM((HEADS, BAND_WIN, BAND_TQ), BF16),
    ]
    blk = (2 * (_nbytes((WIDTH, BAND_TQ), BF16) + _nbytes((BAND_TQ, WIDTH), F32))
           + 2 * _nbytes((s, WIDTH), BF16) + _nbytes(abiasT.shape, F32)
           + _nbytes((HEADS, BAND_WIN, BAND_TQ), F32) + _nbytes((HEADS, BAND_WIN, BAND_TQ), BF16))
    return pl.pallas_call(
        _band_kernel, out_shape=jax.ShapeDtypeStruct((b, s, WIDTH), F32),
        grid=grid, in_specs=in_specs, out_specs=out_specs, scratch_shapes=scratch, name="band",
        compiler_params=pltpu.CompilerParams(
            dimension_semantics=("arbitrary", "arbitrary"), vmem_limit_bytes=_vmem_limit(blk)),
    )(qaT, ka3, vaT4, abiasT)


def _make_dsa_kernel(n_sel):
    T = DSA_T
    kf = float(n_sel)
    idx_scale = (HEADS ** -0.5) * (HEAD_DIM ** -0.5)

    def kernel(qbT_ref, qiT_ref, wiT_ref, ki_ref, kb_ref, vbT_ref, gbias_ref, adm_ref, o_ref,
               sc_ref, m_ref, acc_ref, tmax_ref, s_ref, p_ref, vx_ref, cz_ref):
        j = pl.program_id(1)
        nt = j + 1
        zrow = jnp.zeros((1, T), F32)

        def tile_rows(kt):
            return pl.ds(pl.multiple_of(kt * T, T), T)

        wis = wiT_ref[...] * idx_scale

        def fold8(a):
            return jnp.sum(a.reshape(a.shape[0] // V7X_SUBLANES, V7X_SUBLANES, T), axis=0)

        HT = T // 2

        raw_ref = s_ref.at[0]

        def p1_half(kt_dot, kt_post, half, carry):
            hs = slice(half * HT, (half + 1) * HT)
            if kt_post is not None:
                mn, mx, cgt0, cge0 = carry
                acc = raw_ref[hs, :]
                mn = jnp.minimum(mn, jnp.min(acc, axis=0, keepdims=True))
                mx = jnp.maximum(mx, jnp.max(acc, axis=0, keepdims=True))
                slab = (kt_post == j).astype(jnp.int32) + 2 * (kt_post > j).astype(jnp.int32)
                sc = acc + adm_ref[slab, hs, :]
                sc_ref[pl.ds(pl.multiple_of(kt_post * T + half * HT, HT), HT), :] = sc
                cgt0 = cgt0 + fold8(jnp.where(sc > 0.0, 1.0, 0.0))
                cge0 = cge0 + fold8(jnp.where(sc >= 0.0, 1.0, 0.0))
                if half == 1:
                    cz_ref[kt_post + 1] = cge0 - cgt0
                carry = (mn, mx, cgt0, cge0)
            if kt_dot is not None:
                kt_ld = jnp.minimum(kt_dot, pl.num_programs(1) - 1)
                ki_t = ki_ref[pl.ds(pl.multiple_of(kt_ld * T + half * HT, HT), HT), :]
                acc = jnp.zeros((HT, T), F32)
                for h in range(HEADS):
                    lg = jnp.dot(ki_t, qiT_ref[HEAD_DIM * h:HEAD_DIM * (h + 1), :],
                                 preferred_element_type=F32)
                    acc = acc + jnp.maximum(lg, 0.0) * wis[h:h + 1, :]
                raw_ref[hs, :] = acc
            return carry

        ntp = nt + nt % 2

        def p1_tile(kt_dot, kt_post, carry):
            for half in range(2):
                carry = p1_half(kt_dot, kt_post, half, carry)
            return carry

        def p1(i, carry):
            carry = p1_tile(2 * i + 1, 2 * i, carry)
            return p1_tile(2 * i + 2, 2 * i + 1, carry)

        z8 = jnp.zeros((V7X_SUBLANES, T), F32)
        cz_ref[0] = z8
        carry = (jnp.full((1, T), jnp.inf, F32), jnp.full((1, T), -jnp.inf, F32), z8, z8)
        p1_tile(0, None, None)
        carry = lax.fori_loop(0, ntp // 2 - 1, p1, carry)
        carry = p1_tile(ntp - 1, ntp - 2, carry)
        carry = p1_tile(None, ntp - 1, carry)
        lo0, hi0, cgt0, cge0 = carry
        cgt0 = jnp.sum(cgt0, axis=0, keepdims=True)
        cge0 = jnp.sum(cge0, axis=0, keepdims=True)

        qpos = j * T + lax.broadcasted_iota(jnp.int32, (1, T), 1)
        n_adm = ((qpos // CHUNK + 1) * CHUNK).astype(F32)
        small = n_adm <= kf
        above0 = cgt0 >= kf
        below0 = cge0 < kf
        zero_tie = jnp.logical_not(jnp.logical_or(above0, below0))
        lo_pos = lo0 > 0.0
        lo1 = jnp.where(zero_tie, 0.0, jnp.where(jnp.logical_and(above0, jnp.logical_not(lo_pos)),
                                                 0.0, lo0))
        cl1 = jnp.where(jnp.logical_or(zero_tie, jnp.logical_and(above0, jnp.logical_not(lo_pos))),
                        cge0, n_adm)
        hi1 = jnp.where(zero_tie, 0.0, jnp.where(below0, jnp.minimum(hi0, 0.0), hi0))

        def count_ge(th):
            def body(i, c8):
                for u in range(2):
                    blk = sc_ref[tile_rows(2 * i + u), :]
                    c8 = c8 + fold8(jnp.where(blk >= th, 1.0, 0.0))
                return c8
            return jnp.sum(lax.fori_loop(0, ntp // 2, body, z8), axis=0, keepdims=True)

        def any_open(done):
            return jnp.max(jnp.where(done, 0.0, 1.0)) > 0.0

        def settled(cl_):
            return jnp.logical_or(jnp.logical_or(small, zero_tie), cl_ == kf)

        def bisect(lo, hi, cl):
            mid = 0.5 * lo + 0.5 * hi
            c = count_ge(mid)
            ge = c >= kf
            return jnp.where(ge, mid, lo), jnp.where(ge, hi, mid), jnp.where(ge, c, cl)

        def b_cond(st):
            it, _, _, _, open_ = st
            return jnp.logical_and(it < BISECT_MAX_ITERS, open_)

        def b_body(st):
            it, lo, hi, cl, _ = st
            lo, hi, cl = bisect(lo, hi, cl)
            return it + 1, lo, hi, cl, any_open(settled(cl))

        open1 = any_open(settled(cl1))
        n_blind = jnp.where(open1, BISECT_BLIND_ITERS, 0)
        lo, hi, cl = lax.fori_loop(0, n_blind, lambda _, st: bisect(*st), (lo1, hi1, cl1))
        _, lo, hi, cl, open_ = lax.while_loop(
            b_cond, b_body, (n_blind, lo, hi, cl, jnp.logical_and(open1, any_open(settled(cl)))))
        has_ties = jnp.logical_or(open_, jnp.max(jnp.where(
            jnp.logical_and(zero_tie, jnp.logical_not(small)), 1.0, 0.0)) > 0.0)

        def fast_mask():
            thr = jnp.where(small, F32_MIN, lo)

            def body(kt, _):
                blk = sc_ref[tile_rows(kt), :]
                sc_ref[tile_rows(kt), :] = jnp.where(blk >= thr, 0.0, NEG)
                return 0
            lax.fori_loop(0, nt, body, 0)

        def exact_mask():
            def done_of(lo_, hi_, cl_):
                return jnp.logical_or(jnp.logical_or(small, cl_ == kf), lo_ >= hi_)

            def s_body(st):
                lo_, hi_, cl_, _ = st
                done = done_of(lo_, hi_, cl_)
                mid = 0.5 * lo_ + 0.5 * hi_
                mid = jnp.where(mid > lo_, mid, hi_)

                def body(kt, c3):
                    c, mn_ge, mx_lt = c3
                    blk = sc_ref[tile_rows(kt), :]
                    ge_ = blk >= mid
                    c = c + jnp.sum(jnp.where(ge_, 1.0, 0.0), axis=0, keepdims=True)
                    mn_ge = jnp.minimum(mn_ge, jnp.min(jnp.where(ge_, blk, jnp.inf),
                                                       axis=0, keepdims=True))
                    mx_lt = jnp.maximum(mx_lt, jnp.max(jnp.where(ge_, -jnp.inf, blk),
                                                       axis=0, keepdims=True))
                    return c, mn_ge, mx_lt
                c, mn_ge, mx_lt = lax.fori_loop(
                    0, nt, body, (zrow, jnp.full((1, T), jnp.inf, F32),
                                  jnp.full((1, T), -jnp.inf, F32)))
                ge = c >= kf
                lo_n = jnp.where(done, lo_, jnp.where(ge, mn_ge, lo_))
                cl_n = jnp.where(done, cl_, jnp.where(ge, c, cl_))
                hi_n = jnp.where(done, hi_, jnp.where(ge, hi_, mx_lt))
                return lo_n, hi_n, cl_n, any_open(done_of(lo_n, hi_n, cl_n))

            lo_e, _, cl_e, _ = lax.while_loop(
                lambda st: st[3], s_body, (lo, hi, cl, any_open(done_of(lo, hi, cl))))
            thr = jnp.where(small, F32_MIN, lo_e)

            def count_gt():
                def body(kt, c8):
                    blk = sc_ref[tile_rows(kt), :]
                    return c8 + fold8(jnp.where(blk > thr, 1.0, 0.0))
                return jnp.sum(lax.fori_loop(0, nt, body, z8), axis=0, keepdims=True)
            cgt = lax.cond(open_, count_gt, lambda: cgt0)
            need = jnp.where(jnp.logical_or(small, cl_e == kf), jnp.inf, kf - cgt)

            UNDECIDED = -1.0

            def m_body(kt, st):
                carry, xt, xc = st
                blk = sc_ref[tile_rows(kt), :]
                eq = jnp.where(blk == thr, 1.0, 0.0)
                cnt = jnp.sum(fold8(eq), axis=0, keepdims=True)
                has_budget = carry < need
                crossing = jnp.logical_and(has_budget, carry + cnt > need)
                tie_val = jnp.where(crossing, UNDECIDED, jnp.where(has_budget, 0.0, NEG))
                sc_ref[tile_rows(kt), :] = jnp.where(
                    blk > thr, 0.0, jnp.where(blk == thr, tie_val, NEG))
                xt = jnp.where(crossing, kt.astype(F32), xt)
                xc = jnp.where(crossing, carry, xc)
                return carry + cnt, xt, xc
            def mz_body(kt, st):
                xt, xc = st
                blk = sc_ref[tile_rows(kt), :]
                before = jnp.sum(cz_ref[kt], axis=0, keepdims=True)
                after = jnp.sum(cz_ref[kt + 1], axis=0, keepdims=True)
                has_budget = before < need
                crossing = jnp.logical_and(has_budget, after > need)
                tie_val = jnp.where(crossing, UNDECIDED, jnp.where(has_budget, 0.0, NEG))
                sc_ref[tile_rows(kt), :] = jnp.where(
                    blk > thr, 0.0, jnp.where(blk == thr, tie_val, NEG))
                return jnp.where(crossing, kt.astype(F32), xt), jnp.where(crossing, before, xc)

            no_x = jnp.full((1, T), -1.0, F32)
            xt, xc = lax.cond(
                open_,
                lambda: lax.fori_loop(0, nt, m_body, (zrow, no_x, zrow))[1:],
                lambda: lax.fori_loop(0, nt, mz_body, (no_x, zrow)))

            def r_body(st):
                xt_, ktf = st
                kt = ktf.astype(jnp.int32)
                msk = sc_ref[tile_rows(kt), :]
                und = msk == UNDECIDED
                r_io = lax.broadcasted_iota(jnp.int32, (T, T), 0)
                c_io = lax.broadcasted_iota(jnp.int32, (T, T), 1)
                ltri = jnp.where(c_io < r_io, 1.0, 0.0).astype(BF16)
                rank = jnp.dot(ltri, jnp.where(und, 1.0, 0.0).astype(BF16),
                               preferred_element_type=F32) + xc
                sc_ref[tile_rows(kt), :] = jnp.where(
                    und, jnp.where(rank < need, 0.0, NEG), msk)
                xt_ = jnp.where(xt_ == ktf, -1.0, xt_)
                return xt_, jnp.max(xt_)
            lax.while_loop(lambda st: st[1] >= 0.0, r_body, (xt, jnp.max(xt)))

        lax.cond(has_ties, exact_mask, fast_mask)

        m_ref[...] = jnp.full(m_ref.shape, NEG, F32)
        acc_ref[...] = jnp.zeros(acc_ref.shape, F32)
        vx_ref[HEAD_DIM:, :] = jnp.ones((VX_ROWS - HEAD_DIM, T), BF16)

        def step(kt_a, bias_slab, kt_bc):
            if kt_bc is not None:
                vx_ref[0:HEAD_DIM, :] = vbT_ref[kt_bc]
                m_old = m_ref[...]
                m_new = jnp.maximum(m_old, tmax_ref[...])
                alpha = jnp.exp2(m_old - m_new)
                m_ref[...] = m_new
            if kt_a is not None:
                kb_t = kb_ref[tile_rows(kt_a), :]
                msk = sc_ref[tile_rows(kt_a), :]
            tmax = []
            for h in range(HEADS):
                rows = slice(HEAD_DIM * h, HEAD_DIM * (h + 1))
                if kt_bc is not None:
                    p_ref[h] = jnp.exp2(s_ref[h] - m_new[h:h + 1, :]).astype(BF16)
                if kt_a is not None:
                    add = msk if bias_slab is None else msk + gbias_ref[bias_slab, h]
                    s = jnp.dot(kb_t, qbT_ref[rows, :], preferred_element_type=F32) + add
                    s_ref[h] = s
                    tmax.append(jnp.max(s, axis=0, keepdims=True))
                if kt_bc is not None:
                    for hc in ([h - PV_LAG] if h < HEADS - 1 else range(h - PV_LAG, HEADS)):
                        if hc < 0:
                            continue
                        acc_ref[hc] = alpha[hc:hc + 1, :] * acc_ref[hc] + jnp.dot(
                            vx_ref[...], p_ref[hc], preferred_element_type=F32)
            if kt_a is not None:
                tmax_ref[...] = jnp.concatenate(tmax, axis=0)

        n_far = jnp.maximum(j - 1, 0)
        step(0, jnp.clip(2 - j, 0, 2), None)

        def far_step(a, _):
            step(a, None, a - 1)
            return 0

        def near_step(a, _):
            step(a, a - j + 2, a - 1)
            return 0

        lax.fori_loop(1, n_far, far_step, 0)
        lax.fori_loop(jnp.maximum(n_far, 1), nt, near_step, 0)
        step(None, None, nt - 1)

        outs = []
        for h in range(HEADS):
            a = acc_ref[h]
            outs.append(a[0:HEAD_DIM, :] * (1.0 / a[HEAD_DIM:HEAD_DIM + 1, :]))
        o_ref[...] = jnp.concatenate(outs, axis=0).T

    return kernel


def _dsa(qbT, qiT, wiT, ki3, kb3, vbT4, gbias, adm, n_sel):
    b, s, _ = ki3.shape
    T = DSA_T
    nq = s // T
    grid = (b, nq)
    in_specs = [
        pl.BlockSpec((WIDTH, T), lambda bi, j: (0, bi * nq + j)),
        pl.BlockSpec((WIDTH, T), lambda bi, j: (0, bi * nq + j)),
        pl.BlockSpec((HEADS, T), lambda bi, j: (0, bi * nq + j)),
        pl.BlockSpec((None, s, HEAD_DIM), lambda bi, j: (bi, 0, 0)),
        pl.BlockSpec((None, s, HEAD_DIM), lambda bi, j: (bi, 0, 0)),
        pl.BlockSpec((None, nq, HEAD_DIM, T), lambda bi, j: (bi, 0, 0, 0)),
        pl.BlockSpec(gbias.shape, lambda bi, j: (0, 0, 0, 0)),
        pl.BlockSpec(adm.shape, lambda bi, j: (0, 0, 0)),
    ]
    out_specs = pl.BlockSpec((None, T, WIDTH), lambda bi, j: (bi, j, 0))
    scratch = [
        pltpu.VMEM((s + T, T), F32),
        pltpu.VMEM((HEADS, T), F32),
        pltpu.VMEM((HEADS, VX_ROWS, T), F32),
        pltpu.VMEM((HEADS, T), F32),
        pltpu.VMEM((HEADS, T, T), F32),
        pltpu.VMEM((HEADS, T, T), BF16),
        pltpu.VMEM((VX_ROWS, T), BF16),
        pltpu.VMEM((nq + 2, V7X_SUBLANES, T), F32),
    ]
    blk = (2 * (2 * _nbytes((WIDTH, T), BF16) + _nbytes((HEADS, T), F32)
                + 2 * _nbytes((s, V7X_LANES), BF16) + _nbytes((HEAD_DIM, s), BF16)
                + _nbytes(gbias.shape, F32) + _nbytes(adm.shape, F32)
                + _nbytes((T, WIDTH), F32))
           + _nbytes((s + T, T), F32) + 2 * _nbytes((HEADS, T), F32)
           + _nbytes((HEADS, VX_ROWS, T), F32) + _nbytes((HEADS, T, T), F32)
           + _nbytes((HEADS, T, T), BF16) + _nbytes((VX_ROWS, T), BF16)
           + _nbytes((nq + 2, V7X_SUBLANES, T), F32))
    return pl.pallas_call(
        _make_dsa_kernel(n_sel), out_shape=jax.ShapeDtypeStruct((b, s, WIDTH), F32),
        grid=grid, in_specs=in_specs, out_specs=out_specs, scratch_shapes=scratch, name="dsa",
        compiler_params=pltpu.CompilerParams(
            dimension_semantics=("arbitrary", "arbitrary"), vmem_limit_bytes=_vmem_limit(blk)),
    )(qbT, qiT, wiT, ki3, kb3, vbT4, gbias, adm)


def _merge_kernel(x_ref, ya_ref, yb_ref, gate_ref, wa_ref, wb_ref, wo_ref, fg_ref, o_ref):
    d = x_ref.shape[1]
    za = gate_ref[:, 0:WIDTH].astype(F32)
    zb = gate_ref[:, WIDTH:2 * WIDTH].astype(F32)
    ua = (ya_ref[...] * (za * _sigmoid(za))).astype(BF16)
    ub = (yb_ref[...] * (zb * _sigmoid(zb))).astype(BF16)
    acc = None
    for c0 in range(0, d, MERGE_COLS):
        cs = slice(c0, c0 + MERGE_COLS)
        ga = gate_ref[:, 2 * WIDTH + c0:2 * WIDTH + c0 + MERGE_COLS].astype(F32)
        gb = gate_ref[:, 2 * WIDTH + d + c0:2 * WIDTH + d + c0 + MERGE_COLS].astype(F32)
        pa = jnp.dot(ua, wa_ref[:, cs], preferred_element_type=F32)
        pb = jnp.dot(ub, wb_ref[:, cs], preferred_element_type=F32)
        merged = (_sigmoid(ga) * pa + _sigmoid(gb) * pb).astype(BF16)
        part = jnp.dot(merged, wo_ref[cs, :], preferred_element_type=F32)
        acc = part if acc is None else acc + part
    h = x_ref[...] + acc
    ms = jnp.mean(h * h, axis=-1, keepdims=True)
    o_ref[...] = (h * lax.rsqrt(ms + EPS)) * fg_ref[...]


def _merge(x2, ya2, yb2, gates, wa, wb, wo, fg, tm):
    n, d = x2.shape
    n_gate = gates.shape[1]
    grid = (n // tm,)
    row = lambda i: (i, 0)
    const = lambda i: (0, 0)
    in_specs = [
        pl.BlockSpec((tm, d), row),
        pl.BlockSpec((tm, WIDTH), row),
        pl.BlockSpec((tm, WIDTH), row),
        pl.BlockSpec((tm, n_gate), row),
        pl.BlockSpec(wa.shape, const),
        pl.BlockSpec(wb.shape, const),
        pl.BlockSpec(wo.shape, const),
        pl.BlockSpec((1, d), const),
    ]
    blk = 2 * (2 * _nbytes((tm, d), F32) + 2 * _nbytes((tm, WIDTH), F32)
               + _nbytes((tm, n_gate), BF16) + _nbytes(wa.shape, BF16) + _nbytes(wb.shape, BF16)
               + _nbytes(wo.shape, BF16))
    return pl.pallas_call(
        _merge_kernel, out_shape=jax.ShapeDtypeStruct((n, d), F32),
        grid=grid, in_specs=in_specs, out_specs=pl.BlockSpec((tm, d), row), name="merge",
        compiler_params=pltpu.CompilerParams(
            dimension_semantics=("arbitrary",), vmem_limit_bytes=_vmem_limit(blk)),
    )(x2, ya2, yb2, gates, wa, wb, wo, fg)


def _toeplitz(g, nrows, ncols):
    period = nrows + ncols
    lead = g.shape[:-1]
    p = jnp.concatenate([g[..., ::-1], jnp.zeros(lead + (1,), g.dtype)], axis=-1)
    flat = jnp.tile(p, (1,) * len(lead) + (nrows,))[..., :nrows * (period - 1)]
    x = flat.reshape(lead + (nrows, period - 1))
    return x[..., nrows - 1:nrows - 1 + ncols]


def _band_bias(a_rel_bias):
    pad = A_LEFT_CHUNKS * CHUNK
    rel = np.arange(BAND_TQ + BAND_WIN - 1) - (BAND_WIN - 1) + pad
    idx = np.clip(rel, -A_REL_CLIP, A_REL_CLIP) + A_REL_CLIP
    lo, hi = int(idx[0]), int(idx.max())
    n_flat = int((idx == hi).sum()) - 1
    assert np.array_equal(idx, np.minimum(np.arange(lo, lo + idx.size), hi))
    ab = a_rel_bias.astype(F32)
    g = jnp.concatenate([ab[:, lo:hi + 1], jnp.broadcast_to(ab[:, hi:hi + 1], (HEADS, n_flat))],
                        axis=1)
    bias = _toeplitz(g, BAND_TQ, BAND_WIN)
    qq = np.arange(BAND_TQ)[:, None]
    jj = np.arange(BAND_WIN)[None, :]
    in_band = np.logical_and(jj // CHUNK >= qq // CHUNK, jj // CHUNK <= qq // CHUNK + A_LEFT_CHUNKS)
    bias = jnp.where(jnp.asarray(in_band)[None], bias * LOG2E, NEG)
    return jnp.swapaxes(bias, 1, 2)


def _dsa_bias(t5_bias):
    T = DSA_T
    far = T5_BUCKETS // 2 - 1
    tb = t5_bias.astype(F32)
    slabs = [jnp.zeros((HEADS, T, T), F32)]
    for off in (-T, 0):
        rel = jnp.arange(2 * T - 1, dtype=jnp.int32) - (T - 1) + off
        g = (tb[_t5_bucket(rel)] - tb[far]) * LOG2E
        slabs.append(_toeplitz(g.T, T, T))
    return jnp.stack(slabs)


def _far_bucket_is_constant(s):
    half = T5_BUCKETS // 2
    max_exact = half // 2
    for n in (DSA_T + 1, max(s - 1, DSA_T + 1)):
        large = max_exact + int(math.log(n / max_exact) / math.log(T5_MAX_DIST / max_exact)
                                * (half - max_exact))
        if min(large, half - 1) != half - 1:
            return False
    return True


def kernel(x, norm_gain, w_in, a_rel_bias, t5_bias, w_a_out, w_b_out, w_out, final_gain):
    b, s, d = x.shape
    depth = w_in.shape[0]
    n = b * s
    assert s % DSA_T == 0 and s % BAND_TQ == 0
    assert _far_bucket_is_constant(s)
    tm = 512 if n % 512 == 0 else DSA_T
    n_sel = min(TOPK_MAX, s // 4)

    splits = (WIDTH,) * 4 + (WIDTH, HEAD_DIM, HEAD_DIM, WIDTH) + (WIDTH, HEAD_DIM, HEADS) + (d, d)
    offs = np.concatenate([[0], np.cumsum(splits)])
    assert offs[-1] == w_in.shape[2]

    abias = _band_bias
    gbias = _dsa_bias(t5_bias)
    kk = jnp.arange(DSA_T)[:, None] // CHUNK
    qq = jnp.arange(DSA_T)[None, :] // CHUNK
    adm = jnp.stack([jnp.zeros((DSA_T, DSA_T), F32),
                     jnp.where(kk <= qq, 0.0, -jnp.inf).astype(F32),
                     jnp.full((DSA_T, DSA_T), -jnp.inf, F32)])

    h2 = x.reshape(n, d)
    for l in range(depth):
        w = w_in[l]
        (qa, ka, va, za, qb, kb, vb, zb, qi, ki, wi, ga, gb) = [
            w[:, offs[i]:offs[i + 1]] for i in range(len(splits))]
        wn = jnp.concatenate([ka, ki, kb], axis=1).astype(BF16)
        wg = jnp.concatenate([za, zb, ga, gb], axis=1).astype(BF16)
        qscale = HEAD_DIM ** -0.5 * LOG2E
        wt = jnp.concatenate([qa * qscale, va, qb * qscale, qi, vb, wi,
                              jnp.zeros((d, HEADS), w.dtype)], axis=1).T.astype(BF16)

        ka_o, ki_o, kb_o, gates, qaT, vaT, qbT, qiT, vbT, wiT = _proj(
            h2, norm_gain[l].reshape(1, d).astype(F32), wn, wg, wt, tm)

        ya = _band(qaT, ka_o.reshape(b, s, WIDTH),
                   vaT.reshape(b, s // BAND_KB, WIDTH, BAND_KB), abias(a_rel_bias[l]))
        yb = _dsa(qbT, qiT, wiT, ki_o.reshape(b, s, HEAD_DIM), kb_o.reshape(b, s, HEAD_DIM),
                  vbT.reshape(b, s // DSA_T, HEAD_DIM, DSA_T), gbias, adm, n_sel)

        assert depth == 1
        h2 = _merge(h2, ya.reshape(n, WIDTH), yb.reshape(n, WIDTH), gates,
                    w_a_out[l].astype(BF16), w_b_out[l].astype(BF16), w_out[l].astype(BF16),
                    final_gain.reshape(1, d).astype(F32), tm)
    return h2.reshape(b, s, d)
```
